```python
import math
import jax
import jax.numpy as jnp
from jax import lax
import numpy as np

D_MODEL = 4096
BATCH = 4
SEQ = 2048
DEPTH = 4
DEC_BATCH = 128
DEC_SEQ = 1
PAST_LEN = 8192
PAGE_SIZE = 128

N_MIXERS = 4
D_FF = 11008
NORM_EPS = 1e-6
CONV_W = 4
CHUNK = 64
HG_CHUNK = 32
Q_BLOCK = 128
MB_D_INNER = 2 * D_MODEL
MB_HEADDIM = 64
MB_HEADS = MB_D_INNER // MB_HEADDIM
MB_GROUPS = 8
MB_DSTATE = 128
MB_CONV_DIM = MB_D_INNER + 2 * MB_GROUPS * MB_DSTATE
MB_PROJ = MB_D_INNER + MB_CONV_DIM + MB_HEADS
GD_HEAD_DIM = 128
GD_HEADS = D_MODEL // GD_HEAD_DIM
GD_WIDTH = GD_HEADS * GD_HEAD_DIM
GD_PROJ = 4 * GD_WIDTH + 2 * GD_HEADS
HG_DK = 128
HG_HEADS = D_MODEL // HG_DK
HG_DV = D_MODEL // HG_HEADS
ML_HEADS = 32
ML_Q_RANK = 1024
ML_KV_RANK = 512
ML_NOPE = 128
ML_ROPE = 64
ML_V = 128
ML_ROPE_THETA = 10000.0
ML_ROW = ML_KV_RANK + ML_ROPE
N_MB = len(range(0, DEPTH, N_MIXERS))
N_GD = len(range(1, DEPTH, N_MIXERS))
N_HG = len(range(2, DEPTH, N_MIXERS))
N_ML = len(range(3, DEPTH, N_MIXERS))

kernel_name = 'hybrid_ssd_gdn_hgrn2_mla_macaron_step'

F32 = jnp.float32


def rmsnorm(x, g):
    xf = x.astype(F32)
    y = xf * lax.rsqrt(jnp.mean(xf * xf, axis=-1, keepdims=True) + NORM_EPS)
    return (y * g.astype(F32)).astype(x.dtype)


def l2norm(x):
    return x * lax.rsqrt(jnp.sum(x * x, axis=-1, keepdims=True) + NORM_EPS)


def swiglu(x, w_gate, w_up, w_down):
    return (jax.nn.silu(x @ w_gate) * (x @ w_up)) @ w_down


def causal_conv(x, buf, w):
    t = x.shape[1]
    xp = jnp.concatenate([buf.astype(x.dtype), x], axis=1)
    y = xp[:, 0:t] * w[0]
    for j in range(1, CONV_W):
        y = y + xp[:, j:j + t] * w[j]
    return y, xp[:, t:]


def _to_chunks(a, c):
    b, t = a.shape[0], a.shape[1]
    n = -(-t // c)
    a = jnp.pad(a, [(0, 0), (0, n * c - t)] + [(0, 0)] * (a.ndim - 2))
    return jnp.swapaxes(a.reshape((b, n, c) + a.shape[2:]), 0, 1)


def _from_chunks(y, t):
    y = jnp.swapaxes(y, 0, 1)
    return y.reshape((y.shape[0], -1) + y.shape[3:])[:, :t]


def chunked_scan(step, state, seqs, chunk):
    t = seqs[0].shape[1]
    c = min(chunk, t)
    state, ys = lax.scan(step, state, tuple(_to_chunks(a, c) for a in seqs))
    return _from_chunks(ys, t), state


def _ssd_step(s, inp):
    xs, bm, cm, dt, la = inp
    c = xs.shape[1]
    mask = jnp.tril(jnp.ones((c, c), bool))[None, :, :, None, None]
    cum = jnp.cumsum(la, axis=1)
    decay = jnp.exp(jnp.where(mask, cum[:, :, None] - cum[:, None, :], -jnp.inf))
    w = jnp.einsum('btgn,bsgn->btsg', cm, bm)[..., None] * decay * dt[:, None]
    y = jnp.einsum('btsgh,bsghp->btghp', w, xs)
    y = y + jnp.einsum('btgn,bghpn->btghp', cm, s) * jnp.exp(cum)[..., None]
    tail = jnp.exp(cum[:, -1:] - cum) * dt
    s = s * jnp.exp(cum[:, -1])[..., None, None] + jnp.einsum('bsgh,bsghp,bsgn->bghpn', tail, xs, bm)
    return s, y


def mamba2_mixer(u, conv_buf, state, w_in, conv_w, conv_b, dt_bias, a_log, d_skip, norm_g, w_out):
    b, t, _ = u.shape
    hg = MB_HEADS // MB_GROUPS
    z, xbc, dt = jnp.split(u @ w_in, [MB_D_INNER, MB_D_INNER + MB_CONV_DIM], axis=-1)
    xbc, new_buf = causal_conv(xbc, conv_buf, conv_w)
    xbc = jax.nn.silu(xbc + conv_b).astype(F32)
    xs, bm, cm = jnp.split(xbc, [MB_D_INNER, MB_D_INNER + MB_GROUPS * MB_DSTATE], axis=-1)
    xs = xs.reshape(b, t, MB_GROUPS, hg, MB_HEADDIM)
    bm = bm.reshape(b, t, MB_GROUPS, MB_DSTATE)
    cm = cm.reshape(b, t, MB_GROUPS, MB_DSTATE)
    dt = jax.nn.softplus(dt.astype(F32) + dt_bias.astype(F32))
    la = dt * (-jnp.exp(a_log.astype(F32)))
    dt = dt.reshape(b, t, MB_GROUPS, hg)
    la = la.reshape(b, t, MB_GROUPS, hg)
    s0 = state.astype(F32).reshape(b, MB_GROUPS, hg, MB_HEADDIM, MB_DSTATE)
    y, s = chunked_scan(_ssd_step, s0, (xs, bm, cm, dt, la), CHUNK)
    y = y + xs * d_skip.astype(F32).reshape(MB_GROUPS, hg, 1)
    y = y.reshape(b, t, MB_D_INNER).astype(u.dtype)
    y = rmsnorm(y * jax.nn.silu(z), norm_g)
    s = s.reshape(b, MB_HEADS, MB_HEADDIM, MB_DSTATE).astype(state.dtype)
    return y @ w_out, s, new_buf


def _gdn_step(s, inp):
    q, k, v, beta, lg = inp
    c = q.shape[1]
    q, k, v = (jnp.swapaxes(a, 1, 2) for a in (q, k, v))
    beta = jnp.swapaxes(beta, 1, 2)
    g = jnp.cumsum(jnp.swapaxes(lg, 1, 2), axis=-1)
    diff = g[..., :, None] - g[..., None, :]
    incl = jnp.tril(jnp.ones((c, c), bool))
    strict = jnp.tril(jnp.ones((c, c), bool), -1)
    m = beta[..., None] * jnp.einsum('bhtd,bhsd->bhts', k, k) * jnp.exp(jnp.where(strict, diff, -jnp.inf))
    rhs = jnp.concatenate([v * beta[..., None], k * (beta * jnp.exp(g))[..., None]], axis=-1)
    sol = lax.linalg.triangular_solve(m + jnp.eye(c, dtype=m.dtype), rhs,
                                      left_side=True, lower=True, unit_diagonal=True)
    u, w = jnp.split(sol, [GD_HEAD_DIM], axis=-1)
    v_new = u - jnp.einsum('bhtk,bhkv->bhtv', w, s)
    attn = jnp.einsum('bhtd,bhsd->bhts', q, k) * jnp.exp(jnp.where(incl, diff, -jnp.inf))
    o = jnp.einsum('bhtk,bhkv->bhtv', q * jnp.exp(g)[..., None], s) + jnp.einsum('bhts,bhsv->bhtv', attn, v_new)
    s = s * jnp.exp(g[..., -1])[..., None, None] + jnp.einsum(
        'bhtk,bhtv->bhkv', k * jnp.exp(g[..., -1:] - g)[..., None], v_new)
    return s, jnp.swapaxes(o, 1, 2)


def gated_deltanet_mixer(u, conv_buf, state, w_in, conv_w, dt_bias, a_log, norm_g, w_out):
    b, t, _ = u.shape
    qkv, gate, b_raw, a_raw = jnp.split(
        u @ w_in, [3 * GD_WIDTH, 4 * GD_WIDTH, 4 * GD_WIDTH + GD_HEADS], axis=-1)
    qkv, new_buf = causal_conv(qkv, conv_buf, conv_w)
    qkv = jax.nn.silu(qkv).astype(F32).reshape(b, t, 3, GD_HEADS, GD_HEAD_DIM)
    q = l2norm(qkv[:, :, 0]) * GD_HEAD_DIM ** -0.5
    k = l2norm(qkv[:, :, 1])
    v = qkv[:, :, 2]
    beta = jax.nn.sigmoid(b_raw.astype(F32))
    lg = -jnp.exp(a_log.astype(F32)) * jax.nn.softplus(a_raw.astype(F32) + dt_bias.astype(F32))
    o, s = chunked_scan(_gdn_step, state.astype(F32), (q, k, v, beta, lg), CHUNK)
    o = rmsnorm(o.astype(u.dtype), norm_g) * jax.nn.silu(gate.reshape(b, t, GD_HEADS, GD_HEAD_DIM))
    return o.reshape(b, t, GD_WIDTH) @ w_out, s.astype(state.dtype), new_buf


def _gla_step(s, inp):
    q, k, v, lf = inp
    c = q.shape[1]
    mask = jnp.tril(jnp.ones((c, c), bool))[None, :, :, None, None]
    cum = jnp.cumsum(lf, axis=1)
    decay = jnp.exp(jnp.where(mask, cum[:, :, None] - cum[:, None, :], -jnp.inf))
    attn = jnp.einsum('bthk,bshk,btshk->bhts', q, k, decay)
    o = jnp.einsum('bhts,bshv->bthv', attn, v) + jnp.einsum('bthk,bhkv->bthv', q * jnp.exp(cum), s)
    last = cum[:, -1]
    s = s * jnp.exp(last)[..., None] + jnp.einsum('bshk,bshv->bhkv', k * jnp.exp(last[:, None] - cum), v)
    return s, o


def hgrn2_mixer(u, state, lb, w_in, norm_g, w_out):
    b, t, _ = u.shape
    shp = (b, t, HG_HEADS, HG_DK)
    q, f, i, g = jnp.split(u @ w_in, 4, axis=-1)
    q = jax.nn.silu(q.astype(F32)).reshape(shp) * HG_DK ** -0.5
    fg = lb + (1.0 - lb) * jax.nn.sigmoid(f.astype(F32))
    lf = jnp.log(fg).reshape(shp)
    k = (1.0 - fg).reshape(shp)
    v = i.astype(F32).reshape(b, t, HG_HEADS, HG_DV)
    o, s = chunked_scan(_gla_step, state.astype(F32), (q, k, v, lf), HG_CHUNK)
    o = rmsnorm(o.astype(u.dtype), norm_g) * jax.nn.silu(g.reshape(b, t, HG_HEADS, HG_DV))
    return o.reshape(b, t, D_MODEL) @ w_out, s.astype(state.dtype)


def rope(x, pos):
    half = ML_ROPE // 2
    inv = ML_ROPE_THETA ** (-jnp.arange(half, dtype=F32) / half)
    ang = pos.astype(F32)[:, None] * inv
    bshape = (ang.shape[0],) + (1,) * (x.ndim - 3) + (half,)
    cos, sin = jnp.cos(ang).reshape(bshape), jnp.sin(ang).reshape(bshape)
    xf = x.astype(F32)
    x1, x2 = xf[..., :half], xf[..., half:]
    return jnp.concatenate([x1 * cos - x2 * sin, x2 * cos + x1 * sin], axis=-1).astype(x.dtype)


def mla_project(u, pos, w_a, q_norm, kv_norm, w_qb):
    b, t, _ = u.shape
    cq, ckv, kpe = jnp.split(u @ w_a, [ML_Q_RANK, ML_Q_RANK + ML_KV_RANK], axis=-1)
    q = (rmsnorm(cq, q_norm) @ w_qb).reshape(b, t, ML_HEADS, ML_NOPE + ML_ROPE)
    q_nope, q_pe = q[..., :ML_NOPE], rope(q[..., ML_NOPE:], pos)
    rows = jnp.concatenate([rmsnorm(ckv, kv_norm), rope(kpe, pos)], axis=-1)
    return q_nope, q_pe, rows


def mla_prompt_attention(q_nope, q_pe, rows, w_kvb):
    b, s_len = q_nope.shape[0], q_nope.shape[1]
    blk = min(Q_BLOCK, s_len)
    nblk = s_len // blk
    kv = (rows[..., :ML_KV_RANK] @ w_kvb).reshape(b, s_len, ML_HEADS, ML_NOPE + ML_V)
    k_nope, v = kv[..., :ML_NOPE], kv[..., ML_NOPE:]
    k_pe = rows[..., ML_KV_RANK:]
    scale = (ML_NOPE + ML_ROPE) ** -0.5
    key_pos = jnp.arange(s_len)
    qn = jnp.swapaxes(q_nope.reshape(b, nblk, blk, ML_HEADS, ML_NOPE), 0, 1)
    qp = jnp.swapaxes(q_pe.reshape(b, nblk, blk, ML_HEADS, ML_ROPE), 0, 1)

    def block(args):
        qn_b, qp_b, idx = args
        sc = (jnp.einsum('bqhd,bkhd->bhqk', qn_b, k_nope)
              + jnp.einsum('bqhr,bkr->bhqk', qp_b, k_pe)).astype(F32) * scale
        qpos = idx * blk + jnp.arange(blk)
        sc = jnp.where(key_pos[None, :] <= qpos[:, None], sc, -jnp.inf)
        p = jax.nn.softmax(sc, axis=-1).astype(v.dtype)
        return jnp.einsum('bhqk,bkhv->bqhv', p, v)

    o = lax.map(block, (qn, qp, jnp.arange(nblk)))
    return jnp.swapaxes(o, 0, 1).reshape(b, s_len, ML_HEADS * ML_V)


def mla_sample_attention(q_nope, q_pe, new_rows, cache, layer, page_table, w_kvb):
    b, t = q_nope.shape[0], q_nope.shape[1]
    wkv = w_kvb.reshape(ML_KV_RANK, ML_HEADS, ML_NOPE + ML_V)
    w_uk, w_uv = wkv[..., :ML_NOPE], wkv[..., ML_NOPE:]
    q_lat = jnp.einsum('bthd,rhd->bthr', q_nope, w_uk)
    scale = (ML_NOPE + ML_ROPE) ** -0.5

    def scores(rows):
        return (jnp.einsum('bthr,bkr->bhtk', q_lat, rows[..., :ML_KV_RANK])
                + jnp.einsum('bthe,bke->bhtk', q_pe, rows[..., ML_KV_RANK:])).astype(F32) * scale

    s_new = jnp.where(jnp.tril(jnp.ones((t, t), bool)), scores(new_rows), -jnp.inf)
    m = jnp.max(s_new, axis=-1)
    p = jnp.exp(s_new - m[..., None])
    l = jnp.sum(p, axis=-1)
    acc = jnp.einsum('bhtk,bkr->bhtr', p, new_rows[..., :ML_KV_RANK].astype(F32))

    def page_step(carry, pages):
        m, l, acc = carry
        rows = cache[layer, pages]
        sc = scores(rows)
        m_new = jnp.maximum(m, jnp.max(sc, axis=-1))
        alpha = jnp.exp(m - m_new)
        p = jnp.exp(sc - m_new[..., None])
        l = l * alpha + jnp.sum(p, axis=-1)
        acc = acc * alpha[..., None] + jnp.einsum('bhtk,bkr->bhtr', p, rows[..., :ML_KV_RANK].astype(F32))
        return (m_new, l, acc), None

    (m, l, acc), _ = lax.scan(page_step, (m, l, acc), page_table.T)
    o_lat = (acc / l[..., None]).astype(q_nope.dtype)
    o = jnp.einsum('bhtr,rhv->bthv', o_lat, w_uv)
    return o.reshape(b, t, ML_HEADS * ML_V)


def _trunk(x, pos, st_ssm, st_ssm_conv, st_gdn, st_gdn_conv, st_hgrn, cache_mla, page_table, hg_lb, p):
    bsz = x.shape[0]
    prompt = cache_mla is None
    out_ssm, out_ssm_conv, out_gdn, out_gdn_conv, out_hgrn, out_mla = [], [], [], [], [], []
    h = x
    for i in range(DEPTH):
        mix_id, j = i % N_MIXERS, i // N_MIXERS
        g = p['norm_g'][i]
        h = h + 0.5 * rmsnorm(swiglu(rmsnorm(h, g[0]), p['ffn_w_gate'][i, 0], p['ffn_w_up'][i, 0],
                                     p['ffn_w_down'][i, 0]), g[1])
        u = rmsnorm(h, g[2])
        if mix_id == 0:
            s0 = jnp.zeros((bsz, MB_HEADS, MB_HEADDIM, MB_DSTATE), F32) if prompt else st_ssm[j]
            c0 = jnp.zeros((bsz, CONV_W - 1, MB_CONV_DIM), x.dtype) if prompt else st_ssm_conv[j]
            mix, s1, c1 = mamba2_mixer(u, c0, s0, p['mb_w_in'][j], p['mb_conv_w'][j], p['mb_conv_b'][j],
                                       p['mb_dt_bias'][j], p['mb_a_log'][j], p['mb_d'][j],
                                       p['mb_norm'][j], p['mb_w_out'][j])
            out_ssm.append(s1)
            out_ssm_conv.append(c1)
        elif mix_id == 1:
            s0 = jnp.zeros((bsz, GD_HEADS, GD_HEAD_DIM, GD_HEAD_DIM), F32) if prompt else st_gdn[j]
            c0 = jnp.zeros((bsz, CONV_W - 1, 3 * GD_WIDTH), x.dtype) if prompt else st_gdn_conv[j]
            mix, s1, c1 = gated_deltanet_mixer(u, c0, s0, p['gd_w_in'][j], p['gd_conv_w'][j],
                                               p['gd_dt_bias'][j], p['gd_a_log'][j], p['gd_norm'][j],
                                               p['gd_w_out'][j])
            out_gdn.append(s1)
            out_gdn_conv.append(c1)
        elif mix_id == 2:
            s0 = jnp.zeros((bsz, HG_HEADS, HG_DK, HG_DV), F32) if prompt else st_hgrn[j]
            mix, s1 = hgrn2_mixer(u, s0, hg_lb[i], p['hg_w_in'][j], p['hg_norm'][j], p['hg_w_out'][j])
            out_hgrn.append(s1)
        else:
            q_nope, q_pe, rows = mla_project(u, pos, p['ml_w_a'][j], p['ml_q_norm'][j],
                                             p['ml_kv_norm'][j], p['ml_w_qb'][j])
            if prompt:
                o = mla_prompt_attention(q_nope, q_pe, rows, p['ml_w_kvb'][j])
            else:
                o = mla_sample_attention(q_nope, q_pe, rows, cache_mla, j, page_table, p['ml_w_kvb'][j])
            mix = o @ p['ml_w_o'][j]
            out_mla.append(rows)
        h = h + rmsnorm(mix, g[3])
        h = h + 0.5 * rmsnorm(swiglu(rmsnorm(h, g[4]), p['ffn_w_gate'][i, 1], p['ffn_w_up'][i, 1],
                                     p['ffn_w_down'][i, 1]), g[5])
    return (h, jnp.stack(out_ssm), jnp.stack(out_ssm_conv), jnp.stack(out_gdn), jnp.stack(out_gdn_conv),
            jnp.stack(out_hgrn), jnp.stack(out_mla))


def setup_inputs(seed: int = 0) -> dict:
    key = jax.random.key(seed)
    ks = iter(jax.random.split(key, 48))

    def nrm(shape, scale):
        return jax.random.normal(next(ks), shape, F32) * scale

    def gain(shape):
        return 1.0 + nrm(shape, 0.02)

    def dt_bias(shape):
        dt = jnp.exp(jax.random.uniform(next(ks), shape, F32, math.log(1e-3), math.log(1e-1)))
        return dt + jnp.log(-jnp.expm1(-dt))

    def a_log(shape):
        return jnp.log(jax.random.uniform(next(ks), shape, F32, 1.0, 16.0))

    n_pages = PAST_LEN // PAGE_SIZE
    n_used = DEC_BATCH * n_pages
    n_phys = n_used + (n_used + 3) // 4
    x_prompt = nrm((BATCH, SEQ, D_MODEL), 1.0)
    x_sample = nrm((DEC_BATCH, DEC_SEQ, D_MODEL), 1.0)
    state_ssm = nrm((N_MB, DEC_BATCH, MB_HEADS, MB_HEADDIM, MB_DSTATE), 0.5)
    state_ssm_conv = nrm((N_MB, DEC_BATCH, CONV_W - 1, MB_CONV_DIM), 1.0)
    state_gdn = nrm((N_GD, DEC_BATCH, GD_HEADS, GD_HEAD_DIM, GD_HEAD_DIM), 0.5)
    state_gdn_conv = nrm((N_GD, DEC_BATCH, CONV_W - 1, 3 * GD_WIDTH), 1.0)
    state_hgrn = nrm((N_HG, DEC_BATCH, HG_HEADS, HG_DK, HG_DV), 0.5)
    cache_mla = nrm((N_ML, n_phys, PAGE_SIZE, ML_ROW), 1.0)
    perm = jax.random.permutation(next(ks), n_phys)
    page_table = perm[:n_used].reshape(DEC_BATCH, n_pages).astype(jnp.int32)
    return {
        'x_prompt': x_prompt,
        'x_sample': x_sample,
        'state_ssm': state_ssm,
        'state_ssm_conv': state_ssm_conv,
        'state_gdn': state_gdn,
        'state_gdn_conv': state_gdn_conv,
        'state_hgrn': state_hgrn,
        'cache_mla': cache_mla,
        'page_table': page_table,
        'norm_g': gain((DEPTH, 6, D_MODEL)),
        'ffn_w_gate': nrm((DEPTH, 2, D_MODEL, D_FF), D_MODEL ** -0.5),
        'ffn_w_up': nrm((DEPTH, 2, D_MODEL, D_FF), D_MODEL ** -0.5),
        'ffn_w_down': nrm((DEPTH, 2, D_FF, D_MODEL), D_FF ** -0.5),
        'mb_w_in': nrm((N_MB, D_MODEL, MB_PROJ), D_MODEL ** -0.5),
        'mb_conv_w': nrm((N_MB, CONV_W, MB_CONV_DIM), CONV_W ** -0.5),
        'mb_conv_b': nrm((N_MB, MB_CONV_DIM), 0.01),
        'mb_dt_bias': dt_bias((N_MB, MB_HEADS)),
        'mb_a_log': a_log((N_MB, MB_HEADS)),
        'mb_d': 1.0 + nrm((N_MB, MB_HEADS), 0.1),
        'mb_norm': gain((N_MB, MB_D_INNER)),
        'mb_w_out': nrm((N_MB, MB_D_INNER, D_MODEL), MB_D_INNER ** -0.5),
        'gd_w_in': nrm((N_GD, D_MODEL, GD_PROJ), D_MODEL ** -0.5),
        'gd_conv_w': nrm((N_GD, CONV_W, 3 * GD_WIDTH), CONV_W ** -0.5),
        'gd_dt_bias': dt_bias((N_GD, GD_HEADS)),
        'gd_a_log': a_log((N_GD, GD_HEADS)),
        'gd_norm': gain((N_GD, GD_HEAD_DIM)),
        'gd_w_out': nrm((N_GD, GD_WIDTH, D_MODEL), GD_WIDTH ** -0.5),
        'hg_w_in': nrm((N_HG, D_MODEL, 4 * D_MODEL), D_MODEL ** -0.5),
        'hg_lb_logits': nrm((DEPTH, D_MODEL), 0.1),
        'hg_norm': gain((N_HG, HG_DV)),
        'hg_w_out': nrm((N_HG, D_MODEL, D_MODEL), D_MODEL ** -0.5),
        'ml_w_a': nrm((N_ML, D_MODEL, ML_Q_RANK + ML_ROW), D_MODEL ** -0.5),
        'ml_q_norm': gain((N_ML, ML_Q_RANK)),
        'ml_kv_norm': gain((N_ML, ML_KV_RANK)),
        'ml_w_qb': nrm((N_ML, ML_Q_RANK, ML_HEADS * (ML_NOPE + ML_ROPE)), ML_Q_RANK ** -0.5),
        'ml_w_kvb': nrm((N_ML, ML_KV_RANK, ML_HEADS * (ML_NOPE + ML_V)), ML_KV_RANK ** -0.5),
        'ml_w_o': nrm((N_ML, ML_HEADS * ML_V, D_MODEL), (ML_HEADS * ML_V) ** -0.5),
    }


def reference(x_prompt, x_sample, state_ssm, state_ssm_conv, state_gdn, state_gdn_conv, state_hgrn,
              cache_mla, page_table, norm_g, ffn_w_gate, ffn_w_up, ffn_w_down,
              mb_w_in, mb_conv_w, mb_conv_b, mb_dt_bias, mb_a_log, mb_d, mb_norm, mb_w_out,
              gd_w_in, gd_conv_w, gd_dt_bias, gd_a_log, gd_norm, gd_w_out,
              hg_w_in, hg_lb_logits, hg_norm, hg_w_out,
              ml_w_a, ml_q_norm, ml_kv_norm, ml_w_qb, ml_w_kvb, ml_w_o):
    lb_p = jax.nn.softmax(hg_lb_logits.astype(F32), axis=0)
    hg_lb = jnp.cumsum(lb_p, axis=0) - lb_p[0]
    params = {
        'norm_g': norm_g, 'ffn_w_gate': ffn_w_gate, 'ffn_w_up': ffn_w_up, 'ffn_w_down': ffn_w_down,
        'mb_w_in': mb_w_in, 'mb_conv_w': mb_conv_w, 'mb_conv_b': mb_conv_b, 'mb_dt_bias': mb_dt_bias,
        'mb_a_log': mb_a_log, 'mb_d': mb_d, 'mb_norm': mb_norm, 'mb_w_out': mb_w_out,
        'gd_w_in': gd_w_in, 'gd_conv_w': gd_conv_w, 'gd_dt_bias': gd_dt_bias, 'gd_a_log': gd_a_log,
        'gd_norm': gd_norm, 'gd_w_out': gd_w_out,
        'hg_w_in': hg_w_in, 'hg_norm': hg_norm, 'hg_w_out': hg_w_out,
        'ml_w_a': ml_w_a, 'ml_q_norm': ml_q_norm, 'ml_kv_norm': ml_kv_norm, 'ml_w_qb': ml_w_qb,
        'ml_w_kvb': ml_w_kvb, 'ml_w_o': ml_w_o,
    }
    pos_p = jnp.arange(x_prompt.shape[1])
    past = page_table.shape[1] * PAGE_SIZE
    pos_s = past + jnp.arange(x_sample.shape[1])
    y_prompt, ssm_p, ssm_conv_p, gdn_p, gdn_conv_p, hgrn_p, mla_rows_p = _trunk(
        x_prompt, pos_p, None, None, None, None, None, None, None, hg_lb, params)
    y_sample, ssm_s, ssm_conv_s, gdn_s, gdn_conv_s, hgrn_s, mla_rows_s = _trunk(
        x_sample, pos_s, state_ssm, state_ssm_conv, state_gdn, state_gdn_conv, state_hgrn,
        cache_mla, page_table, hg_lb, params)
    return (y_prompt, y_sample, ssm_p, ssm_conv_p, gdn_p, gdn_conv_p, hgrn_p, mla_rows_p,
            ssm_s, ssm_conv_s, gdn_s, gdn_conv_s, hgrn_s, mla_rows_s)
```

```python
import functools
import math

import jax
import jax.numpy as jnp
from jax import lax
from jax.experimental import pallas as pl
from jax.experimental.pallas import tpu as pltpu

F32 = jnp.float32
BF16 = jnp.bfloat16
HIGHEST = lax.Precision.HIGHEST

LANES = 128
VMEM_LIMIT_BYTES = 60 * 1024 * 1024

NORM_EPS = 1e-6
CONV_W = 4
CHUNK = 64
HG_CHUNK = 32
MB_HEADDIM = 64
MB_GROUPS = 8
ML_HEADS = 32
ML_NOPE = 128
ML_V = 128
ML_ROPE_THETA = 10000.0


def _cparams(sem):
    return pltpu.CompilerParams(dimension_semantics=sem, vmem_limit_bytes=VMEM_LIMIT_BYTES)


def _silu(x):
    return x * jax.nn.sigmoid(x)


def _softplus(x):
    return jnp.maximum(x, 0.0) + jnp.log1p(jnp.exp(-jnp.abs(x)))


def _bdot(a, b):
    return jnp.dot(a.astype(BF16), b.astype(BF16), preferred_element_type=F32)


def _bdot_nt(a, b):
    return lax.dot_general(a.astype(BF16), b.astype(BF16), (((1,), (1,)), ((), ())),
                           preferred_element_type=F32)


def _fdot(a, b):
    return jnp.dot(a, b, preferred_element_type=F32, precision=HIGHEST)


def _rms(x, g):
    return x * lax.rsqrt(jnp.mean(x * x, axis=-1, keepdims=True) + NORM_EPS) * g


def _prenorm_kernel(x_ref, g_ref, o_ref):
    o_ref[...] = _rms(x_ref[...].astype(F32), g_ref[...]).astype(o_ref.dtype)


def prenorm(x, g, out_dtype=BF16, col_block=0, width=None, tm=256):
    m = x.shape[0]
    d = width or x.shape[1]
    tm = min(tm, m)
    return pl.pallas_call(
        _prenorm_kernel,
        out_shape=jax.ShapeDtypeStruct((m, d), out_dtype),
        grid=(m // tm,),
        in_specs=[pl.BlockSpec((tm, d), lambda i: (i, col_block)),
                  pl.BlockSpec((1, d), lambda i: (0, 0))],
        out_specs=pl.BlockSpec((tm, d), lambda i: (i, 0)),
        compiler_params=_cparams(("parallel",)),
        name="prenorm",
    )(x, g.reshape(1, d))


def _sandwich_kernel(h_ref, y_ref, gpost_ref, gpre_ref, hn_ref, x_ref, *, scale):
    hn = h_ref[...] + scale * _rms(y_ref[...], gpost_ref[...])
    hn_ref[...] = hn
    x_ref[...] = _rms(hn, gpre_ref[...]).astype(x_ref.dtype)


def _sandwich_last_kernel(h_ref, y_ref, gpost_ref, hn_ref, *, scale):
    hn_ref[...] = h_ref[...] + scale * _rms(y_ref[...], gpost_ref[...])


def sandwich(h, y, g_post, g_pre, scale, tm=128):
    m, d = h.shape
    tm = min(tm, m)
    row = pl.BlockSpec((tm, d), lambda i: (i, 0))
    vec = pl.BlockSpec((1, d), lambda i: (0, 0))
    if g_pre is None:
        return pl.pallas_call(
            functools.partial(_sandwich_last_kernel, scale=scale),
            out_shape=jax.ShapeDtypeStruct((m, d), F32),
            grid=(m // tm,), in_specs=[row, row, vec], out_specs=row,
            compiler_params=_cparams(("parallel",)), name="sandwich_last",
        )(h, y, g_post.reshape(1, d)), None
    return pl.pallas_call(
        functools.partial(_sandwich_kernel, scale=scale),
        out_shape=(jax.ShapeDtypeStruct((m, d), F32), jax.ShapeDtypeStruct((m, d), BF16)),
        grid=(m // tm,), in_specs=[row, row, vec, vec], out_specs=(row, row),
        compiler_params=_cparams(("parallel",)), name="sandwich",
    )(h, y, g_post.reshape(1, d), g_pre.reshape(1, d))


def _ffn_kernel(x_ref, wg_ref, wu_ref, wd_ref, o_ref, *, kc, nc):
    j = pl.program_id(1)
    d = x_ref.shape[1]
    g = None
    u = None
    for k0 in range(0, d, kc):
        xk = x_ref[:, k0:k0 + kc]
        pg = jnp.dot(xk, wg_ref[k0:k0 + kc, :].astype(BF16), preferred_element_type=F32)
        pu = jnp.dot(xk, wu_ref[k0:k0 + kc, :].astype(BF16), preferred_element_type=F32)
        g = pg if g is None else g + pg
        u = pu if u is None else u + pu
    hmid = (_silu(g) * u).astype(BF16)

    @pl.when(j == 0)
    def _():
        for n0 in range(0, d, nc):
            o_ref[:, n0:n0 + nc] = jnp.dot(hmid, wd_ref[:, n0:n0 + nc].astype(BF16),
                                           preferred_element_type=F32)

    @pl.when(j > 0)
    def _():
        for n0 in range(0, d, nc):
            o_ref[:, n0:n0 + nc] += jnp.dot(hmid, wd_ref[:, n0:n0 + nc].astype(BF16),
                                            preferred_element_type=F32)


def ffn(x, wg, wu, wd, layer, which, tm=1024, tf=256):
    m, d = x.shape
    f = wg.shape[-1]
    tm = min(tm, m)
    kc = min(1024, d)
    nc = min(512, d)
    return pl.pallas_call(
        functools.partial(_ffn_kernel, kc=kc, nc=nc),
        out_shape=jax.ShapeDtypeStruct((m, d), F32),
        grid=(m // tm, f // tf),
        in_specs=[
            pl.BlockSpec((tm, d), lambda i, j: (i, 0), pipeline_mode=pl.Buffered(1)),
            pl.BlockSpec((None, None, d, tf), lambda i, j: (layer, which, 0, j)),
            pl.BlockSpec((None, None, d, tf), lambda i, j: (layer, which, 0, j)),
            pl.BlockSpec((None, None, tf, d), lambda i, j: (layer, which, j, 0)),
        ],
        out_specs=pl.BlockSpec((tm, d), lambda i, j: (i, 0), pipeline_mode=pl.Buffered(1)),
        compiler_params=_cparams(("parallel", "arbitrary")),
        name="ffn",
    )(x, wg, wu, wd)


def _mm_kernel(x_ref, w_ref, o_ref, *, kc):
    k = x_ref.shape[1]
    acc = None
    for k0 in range(0, k, kc):
        p = jnp.dot(x_ref[:, k0:k0 + kc].astype(BF16), w_ref[k0:k0 + kc, :].astype(BF16),
                    preferred_element_type=F32)
        acc = p if acc is None else acc + p
    o_ref[...] = acc.astype(o_ref.dtype)


def mm(x, w, *, n, tn, col_block=0, tm=1024, out_dtype=F32, lead=None):
    m, k = x.shape
    tm = min(tm, m)
    lead = tuple(lead or ())
    assert w.ndim == 2 + len(lead) and w.shape[-2] == k and n % tn == 0 and m % tm == 0
    wspec = pl.BlockSpec((None,) * len(lead) + (k, tn), lambda i, j: lead + (0, j + col_block))
    return pl.pallas_call(
        functools.partial(_mm_kernel, kc=min(1024, k)),
        out_shape=jax.ShapeDtypeStruct((m, n), out_dtype),
        grid=(m // tm, n // tn),
        in_specs=[pl.BlockSpec((tm, k), lambda i, j: (i, 0)), wspec],
        out_specs=pl.BlockSpec((tm, tn), lambda i, j: (i, j)),
        compiler_params=_cparams(("parallel", "arbitrary")),
        name="mm",
    )(x, w)


def _iota(shape, dim):
    return lax.broadcasted_iota(jnp.int32, shape, dim)


def _conv_silu(x, w_ref, b_ref=None):
    row = _iota(x.shape, 0)
    acc = x * w_ref[CONV_W - 1:CONV_W, :]
    for k in range(1, CONV_W):
        shifted = jnp.where(row >= k, pltpu.roll(x, k, 0), 0.0)
        acc = acc + shifted * w_ref[CONV_W - 1 - k:CONV_W - k, :]
    if b_ref is not None:
        acc = acc + b_ref[...]
    return _silu(acc)


def _chunk_cumsum(x, chunk):
    pos = _iota(x.shape, 0) & (chunk - 1)
    k = 1
    while k < chunk:
        x = x + jnp.where(pos >= k, pltpu.roll(x, k, 0), 0.0)
        k *= 2
    return x


def _lane_select_matrix(sel):
    return (_iota((LANES, LANES), 0) == sel).astype(F32)


def _ssd_kernel(xs_ref, bm_ref, cm_ref, wx_ref, wb_ref, wc_ref, bx_ref, bb_ref, bc_ref,
                dt_ref, dtb_ref, alog_ref, dsk_ref, z_ref, y_ref, st_ref,
                xs_s, bm_s, cm_s, cum_s, dte_s):
    hp = pl.program_id(1)
    t_len = xs_ref.shape[1]
    half = LANES // 2
    xs_s[...] = _conv_silu(xs_ref[0], wx_ref, bx_ref)
    bm_s[...] = _conv_silu(bm_ref[0], wb_ref, bb_ref)
    cm_s[...] = _conv_silu(cm_ref[0], wc_ref, bc_ref)
    dt = _softplus(dt_ref[0] + dtb_ref[...])
    la = dt * (-jnp.exp(alog_ref[...]))
    lane_row = _iota((1, LANES), 1)
    pick = _lane_select_matrix(2 * hp + (lane_row >= half).astype(jnp.int32))
    dte_s[...] = _fdot(dt, pick)
    cum_s[...] = _chunk_cumsum(_fdot(la, pick), CHUNK)

    t_io = _iota((CHUNK, LANES), 0)
    s_io = _iota((CHUNK, LANES), 1) & (half - 1)
    causal = s_io <= t_io
    same_head = (_iota((LANES, LANES), 0) >= half) == (_iota((LANES, LANES), 1) >= half)
    low = lane_row < half
    zeros = jnp.zeros((CHUNK, LANES), F32)
    dskip = dsk_ref[...]

    def pair_of_chunks(i, st):
        r0 = pl.multiple_of(i * (2 * CHUNK), 2 * CHUNK)
        rows = pl.ds(r0, 2 * CHUNK)
        x2, b2, c2, cu2, dt2 = xs_s[rows, :], bm_s[rows, :], cm_s[rows, :], cum_s[rows, :], dte_s[rows, :]
        cu_t, dt_t, b_t = cu2.T, dt2.T, b2.T
        ys = []
        for cc in range(2):
            sl = slice(cc * CHUNK, (cc + 1) * CHUNK)
            xc, bc, cmc, ce, de = x2[sl], b2[sl], c2[sl], cu2[sl], dt2[sl]

            def row_form(m_t):
                a0, a1 = m_t[0:1, :], m_t[half:half + 1, :]
                if cc == 0:
                    return jnp.where(low, a0, pltpu.roll(a1, half, 1))
                return jnp.where(low, pltpu.roll(a0, half, 1), a1)

            cum_row, dt_row = row_form(cu_t), row_form(dt_t)
            cb = _bdot_nt(cmc, jnp.concatenate([bc, bc], axis=0))
            wp = cb * jnp.exp(jnp.where(causal, ce - cum_row, -jnp.inf)) * dt_row
            bd = jnp.where(same_head, jnp.concatenate([xc, xc], axis=0), 0.0)
            y = _bdot(wp, bd) + jnp.exp(ce) * _bdot(cmc, st) + xc * dskip
            ys.append(y)
            cl = ce[CHUNK - 1:CHUNK, :]
            tx = jnp.exp(cl - ce) * de * xc
            tx_pad = jnp.concatenate([tx, zeros] if cc == 0 else [zeros, tx], axis=0)
            st = st * jnp.exp(cl) + _bdot(b_t, tx_pad)
        y_ref[0, rows, :] = jnp.concatenate(ys, axis=0) * _silu(z_ref[0, rows, :])
        return st

    st = lax.fori_loop(0, t_len // (2 * CHUNK), pair_of_chunks, jnp.zeros((LANES, LANES), F32))
    st_ref[0] = st.T


def ssd_prompt(xbc, dt_raw, z, conv_w, conv_b, dt_bias, a_log, d_skip):
    b, t, conv_dim = xbc.shape
    d_inner = z.shape[-1]
    heads = d_inner // MB_HEADDIM
    n = (conv_dim - d_inner) // (2 * MB_GROUPS)
    assert heads == LANES and n == LANES and t % (2 * CHUNK) == 0
    pairs = heads // 2
    ppg = pairs // MB_GROUPS
    xoff, boff = 0, d_inner // LANES
    coff = boff + MB_GROUPS

    def seq(off, grouped):
        if grouped:
            return lambda bi, hp: (bi, 0, off + hp // ppg)
        return lambda bi, hp: (bi, 0, off + hp)

    def par(off, grouped):
        if grouped:
            return lambda bi, hp: (0, off + hp // ppg)
        return lambda bi, hp: (0, off + hp)

    slab = lambda f: pl.BlockSpec((1, t, LANES), f)
    wsp = lambda f: pl.BlockSpec((CONV_W, LANES), f)
    vsp = lambda f: pl.BlockSpec((1, LANES), f)
    cb2 = conv_b.reshape(1, conv_dim)
    dsk = jnp.repeat(d_skip, MB_HEADDIM).reshape(1, d_inner)
    return pl.pallas_call(
        _ssd_kernel,
        out_shape=(jax.ShapeDtypeStruct((b, t, d_inner), F32),
                   jax.ShapeDtypeStruct((b, d_inner, n), F32)),
        grid=(b, pairs),
        in_specs=[slab(seq(xoff, False)), slab(seq(boff, True)), slab(seq(coff, True)),
                  wsp(par(xoff, False)), wsp(par(boff, True)), wsp(par(coff, True)),
                  vsp(par(xoff, False)), vsp(par(boff, True)), vsp(par(coff, True)),
                  slab(lambda bi, hp: (bi, 0, 0)), vsp(lambda bi, hp: (0, 0)), vsp(lambda bi, hp: (0, 0)),
                  vsp(lambda bi, hp: (0, hp)), slab(lambda bi, hp: (bi, 0, hp))],
        out_specs=(slab(lambda bi, hp: (bi, 0, hp)),
                   pl.BlockSpec((1, LANES, LANES), lambda bi, hp: (bi, hp, 0))),
        scratch_shapes=[pltpu.VMEM((t, LANES), F32)] * 5,
        compiler_params=_cparams(("parallel", "arbitrary")),
        name="ssd_prompt",
    )(xbc, xbc, xbc, conv_w, conv_w, conv_w, cb2, cb2, cb2,
      dt_raw, dt_bias.reshape(1, heads), a_log.reshape(1, heads), dsk, z)


def mamba_prompt(u, b, t, j, w_in, conv_w, conv_b, dt_bias, a_log, d_skip, norm_g, w_out):
    m, d = u.shape
    d_inner = norm_g.shape[-1]
    heads = dt_bias.shape[-1]
    conv_dim = conv_w.shape[-1]
    z = mm(u, w_in, n=d_inner, tn=512, lead=(j,))
    xbc = mm(u, w_in, n=conv_dim, tn=512, col_block=d_inner // 512, lead=(j,))
    dt_raw = mm(u, w_in, n=heads, tn=LANES, col_block=(d_inner + conv_dim) // LANES, lead=(j,))
    xbc3 = xbc.reshape(b, t, conv_dim)
    gated, state = ssd_prompt(xbc3, dt_raw.reshape(b, t, heads), z.reshape(b, t, d_inner),
                              conv_w[j], conv_b[j], dt_bias[j], a_log[j], d_skip[j])
    yn = prenorm(gated.reshape(m, d_inner), norm_g[j], tm=128)
    mix = mm(yn, w_out, n=d, tn=256, tm=512, lead=(j,))
    n = state.shape[-1]
    return mix, state.reshape(b, heads, MB_HEADDIM, n), xbc3[:, t - (CONV_W - 1):, :]


def _l2norm(x):
    return x * lax.rsqrt(jnp.sum(x * x, axis=-1, keepdims=True) + NORM_EPS)


def _gdn_kernel(q_ref, k_ref, v_ref, gate_ref, wq_ref, wk_ref, wv_ref, ba_ref, dtb_ref, alog_ref,
                g_ref, o_ref, st_ref, q_s, k_s, v_s, beta_s, cum_s, *, heads):
    h = pl.program_id(1)
    t_len = q_ref.shape[1]
    dk = q_ref.shape[2]
    q_s[...] = _l2norm(_conv_silu(q_ref[0], wq_ref)) * (dk ** -0.5)
    k_s[...] = _l2norm(_conv_silu(k_ref[0], wk_ref))
    v_s[...] = _conv_silu(v_ref[0], wv_ref)
    ba = ba_ref[0]
    beta_all = jax.nn.sigmoid(ba)
    lg_all = -jnp.exp(alog_ref[...]) * _softplus(ba + dtb_ref[...])
    beta_s[...] = _fdot(beta_all, _lane_select_matrix(h))
    cum_s[...] = _chunk_cumsum(_fdot(lg_all, _lane_select_matrix(heads + h)), CHUNK)

    blk = 2 * CHUNK
    r_io, c_io = _iota((blk, blk), 0), _iota((blk, blk), 1)
    same_chunk = (r_io >= CHUNK) == (c_io >= CHUNK)
    strict = same_chunk & (c_io < r_io)
    incl = same_chunk & (c_io <= r_io)
    eye = (r_io == c_io).astype(F32)
    lane_row = _iota((1, blk), 1)
    zeros = jnp.zeros((CHUNK, dk), F32)

    def pair_of_chunks(i, st):
        rows = pl.ds(pl.multiple_of(i * blk, blk), blk)
        q2, k2, v2, be2, g2 = q_s[rows, :], k_s[rows, :], v_s[rows, :], beta_s[rows, :], cum_s[rows, :]
        g_row = g2.T[0:1, :]
        k_t = k2.T
        diff = g2 - g_row
        m = be2 * _bdot_nt(k2, k2) * jnp.exp(jnp.where(strict, diff, -jnp.inf))
        p = -m
        t_inv = eye + p
        for _ in range(5):
            p = _fdot(p, p)
            t_inv = t_inv + _fdot(t_inv, p)
        u2 = _fdot(t_inv, v2 * be2)
        w2 = _fdot(t_inv, k2 * (be2 * jnp.exp(g2)))
        attn = _bdot_nt(q2, k2) * jnp.exp(jnp.where(incl, diff, -jnp.inf))
        qe2 = q2 * jnp.exp(g2)
        vns, o_inter = [], []
        for cc in range(2):
            sl = slice(cc * CHUNK, (cc + 1) * CHUNK)
            vn = u2[sl] - _bdot(w2[sl], st)
            o_inter.append(_bdot(qe2[sl], st))
            g_last = g2[(cc + 1) * CHUNK - 1:(cc + 1) * CHUNK, :]
            in_chunk = (lane_row >= cc * CHUNK) & (lane_row < (cc + 1) * CHUNK)
            kd = k_t * jnp.where(in_chunk, jnp.exp(g_last - g_row), 0.0)
            vn_pad = jnp.concatenate([vn, zeros] if cc == 0 else [zeros, vn], axis=0)
            st = st * jnp.exp(g_last) + _bdot(kd, vn_pad)
            vns.append(vn)
        o2 = jnp.concatenate(o_inter, axis=0) + _bdot(attn, jnp.concatenate(vns, axis=0))
        o_ref[0, rows, :] = (_rms(o2, g_ref[...]) * _silu(gate_ref[0, rows, :])).astype(o_ref.dtype)
        return st

    st = lax.fori_loop(0, t_len // blk, pair_of_chunks, jnp.zeros((dk, dk), F32))
    st_ref[0, 0] = st


def gdn_prompt(qkvg, ba, conv_w, dt_bias, a_log, norm_g):
    b, t, w4 = qkvg.shape
    width = w4 // 4
    dk = norm_g.shape[-1]
    heads = width // dk
    assert dk == LANES and 2 * heads <= LANES and t % (2 * CHUNK) == 0
    pad = LANES - 2 * heads
    dtb = jnp.pad(dt_bias, (heads, pad)).reshape(1, LANES)
    alog = jnp.pad(a_log, (heads, pad)).reshape(1, LANES)
    slab = lambda off: pl.BlockSpec((1, t, dk), lambda bi, h: (bi, 0, off + h))
    wsp = lambda off: pl.BlockSpec((CONV_W, dk), lambda bi, h: (0, off + h))
    vec = pl.BlockSpec((1, LANES), lambda bi, h: (0, 0))
    return pl.pallas_call(
        functools.partial(_gdn_kernel, heads=heads),
        out_shape=(jax.ShapeDtypeStruct((b, t, width), BF16),
                   jax.ShapeDtypeStruct((b, heads, dk, dk), F32)),
        grid=(b, heads),
        in_specs=[slab(0), slab(heads), slab(2 * heads), slab(3 * heads),
                  wsp(0), wsp(heads), wsp(2 * heads),
                  pl.BlockSpec((1, t, LANES), lambda bi, h: (bi, 0, 0)), vec, vec, vec],
        out_specs=(pl.BlockSpec((1, t, dk), lambda bi, h: (bi, 0, h)),
                   pl.BlockSpec((1, 1, dk, dk), lambda bi, h: (bi, h, 0, 0))),
        scratch_shapes=[pltpu.VMEM((t, dk), F32)] * 5,
        compiler_params=_cparams(("parallel", "arbitrary")),
        name="gdn_prompt",
    )(qkvg, qkvg, qkvg, qkvg, conv_w, conv_w, conv_w, ba, dtb, alog, norm_g.reshape(1, dk))


def _tail_cols(w, start, lead):
    cols = w[lead][:, start:]
    return jnp.pad(cols, ((0, 0), (0, LANES - cols.shape[1])))


def gdn_prompt_mixer(u, b, t, j, w_in, conv_w, dt_bias, a_log, norm_g, w_out):
    m, d = u.shape
    width = conv_w.shape[-1] // 3
    qkvg = mm(u, w_in, n=4 * width, tn=512, lead=(j,))
    ba = mm(u, _tail_cols(w_in, 4 * width, j), n=LANES, tn=LANES)
    o, state = gdn_prompt(qkvg.reshape(b, t, 4 * width), ba.reshape(b, t, LANES), conv_w[j],
                          dt_bias[j], a_log[j], norm_g[j])
    mix = mm(o.reshape(m, width), w_out, n=d, tn=512, lead=(j,))
    new_buf = qkvg.reshape(b, t, 4 * width)[:, t - (CONV_W - 1):, :3 * width]
    return mix, state, new_buf


def _hgrn_lower_bound(lb_ref, layer):
    logits = lb_ref[...]
    e = jnp.exp(logits - jnp.max(logits, axis=0, keepdims=True))
    p = e / jnp.sum(e, axis=0, keepdims=True)
    lb = jnp.zeros((1, logits.shape[1]), F32)
    for r in range(1, layer + 1):
        lb = lb + p[r:r + 1, :]
    return lb


def _hgrn_kernel(q_ref, f_ref, i_ref, g_ref, lb_ref, ng_ref, o_ref, st_ref,
                 q_s, k_s, cum_s, *, layer):
    t_len = q_ref.shape[1]
    dk = q_ref.shape[2]
    lb = _hgrn_lower_bound(lb_ref, layer)
    fg = lb + (1.0 - lb) * jax.nn.sigmoid(f_ref[0])
    q_s[...] = _silu(q_ref[0]) * (dk ** -0.5)
    k_s[...] = 1.0 - fg
    cum_s[...] = _chunk_cumsum(jnp.log(fg), HG_CHUNK)

    blk = 4 * HG_CHUNK
    t_io = _iota((HG_CHUNK, dk), 0)
    lane = _iota((HG_CHUNK, blk), 1)
    row_blk = _iota((blk, dk), 0)

    def block_of_chunks(i, st_t):
        r0 = pl.multiple_of(i * blk, blk)
        rows = pl.ds(r0, blk)
        v4 = i_ref[0, rows, :]
        v4_t = v4.T
        outs = []
        for cc in range(4):
            c0 = cc * HG_CHUNK
            crow = pl.ds(r0 + c0, HG_CHUNK)
            qc, kc, cu = q_s[crow, :], k_s[crow, :], cum_s[crow, :]
            attn = jnp.zeros((HG_CHUNK, blk), F32)
            for s in range(HG_CHUNK):
                srow = pl.ds(r0 + c0 + s, 1)
                decay = jnp.exp(jnp.where(t_io >= s, cu - cum_s[srow, :], -jnp.inf))
                col = jnp.sum(qc * k_s[srow, :] * decay, axis=1, keepdims=True)
                attn = jnp.where(lane == c0 + s, col, attn)
            o = _bdot(attn, v4) + _bdot_nt(qc * jnp.exp(cu), st_t)
            outs.append(o)
            last = cu[HG_CHUNK - 1:HG_CHUNK, :]
            in_chunk = (row_blk >= c0) & (row_blk < c0 + HG_CHUNK)
            kd = jnp.where(in_chunk, k_s[rows, :] * jnp.exp(last - cum_s[rows, :]), 0.0)
            st_t = st_t * jnp.exp(last) + _bdot(v4_t, kd)
        o4 = jnp.concatenate(outs, axis=0)
        o_ref[0, rows, :] = (_rms(o4, ng_ref[...]) * _silu(g_ref[0, rows, :])).astype(o_ref.dtype)
        return st_t

    st_t = lax.fori_loop(0, t_len // blk, block_of_chunks, jnp.zeros((dk, dk), F32))
    st_ref[0, 0] = st_t.T


def hgrn_prompt(qfig, lb_logits, norm_g, layer):
    b, t, d4 = qfig.shape
    d = d4 // 4
    dk = norm_g.shape[-1]
    heads = d // dk
    depth = lb_logits.shape[0]
    assert dk == LANES and t % (4 * HG_CHUNK) == 0
    slab = lambda off: pl.BlockSpec((1, t, dk), lambda bi, h: (bi, 0, off + h))
    return pl.pallas_call(
        functools.partial(_hgrn_kernel, layer=layer),
        out_shape=(jax.ShapeDtypeStruct((b, t, d), BF16),
                   jax.ShapeDtypeStruct((b, heads, dk, dk), F32)),
        grid=(b, heads),
        in_specs=[slab(0), slab(heads), slab(2 * heads), slab(3 * heads),
                  pl.BlockSpec((depth, dk), lambda bi, h: (0, h)),
                  pl.BlockSpec((1, dk), lambda bi, h: (0, 0))],
        out_specs=(pl.BlockSpec((1, t, dk), lambda bi, h: (bi, 0, h)),
                   pl.BlockSpec((1, 1, dk, dk), lambda bi, h: (bi, h, 0, 0))),
        scratch_shapes=[pltpu.VMEM((t, dk), F32)] * 3,
        compiler_params=_cparams(("parallel", "arbitrary")),
        name="hgrn_prompt",
    )(qfig, qfig, qfig, qfig, lb_logits, norm_g.reshape(1, dk))


def hgrn_prompt_mixer(u, b, t, j, layer, w_in, lb_logits, norm_g, w_out):
    m, d = u.shape
    qfig = mm(u, w_in, n=4 * d, tn=512, lead=(j,))
    o, state = hgrn_prompt(qfig.reshape(b, t, 4 * d), lb_logits, norm_g[j], layer)
    mix = mm(o.reshape(m, d), w_out, n=d, tn=512, lead=(j,))
    return mix, state


def rope_tables(pos, rope_dim):
    half = rope_dim // 2
    inv = ML_ROPE_THETA ** (-jnp.arange(half, dtype=F32) / half)
    ang = pos.astype(F32)[:, None] * inv
    cos, sin = jnp.cos(ang), jnp.sin(ang)
    reps = LANES // rope_dim
    return (jnp.tile(jnp.concatenate([cos, cos], axis=-1), (1, reps)),
            jnp.tile(jnp.concatenate([-sin, sin], axis=-1), (1, reps)))


def _rope_kernel(x_ref, cos_ref, sin_ref, o_ref, *, rope_dim, dup):
    half = rope_dim // 2
    lane = _iota((x_ref.shape[0], LANES), 1)
    first = (lane & (rope_dim - 1)) < half
    cos, sin = cos_ref[...], sin_ref[...]
    for c in range(x_ref.shape[1] // LANES):
        x = x_ref[:, c * LANES:(c + 1) * LANES]
        other = jnp.where(first, pltpu.roll(x, LANES - half, 1), pltpu.roll(x, half, 1))
        y = x * cos + other * sin
        if dup:
            y = y + pltpu.roll(y, rope_dim, 1)
        o_ref[:, c * LANES:(c + 1) * LANES] = y.astype(o_ref.dtype)


def rope(x, cos, sin, rope_dim, out_dtype, dup=False, tm=256):
    m, w = x.shape
    t = cos.shape[0]
    if t == 1:
        tm = min(tm, m)
        tab = pl.BlockSpec((1, LANES), lambda i: (0, 0))
    else:
        tm = min(tm, t)
        nt = t // tm
        tab = pl.BlockSpec((tm, LANES), lambda i: (i % nt, 0))
    return pl.pallas_call(
        functools.partial(_rope_kernel, rope_dim=rope_dim, dup=dup),
        out_shape=jax.ShapeDtypeStruct((m, w), out_dtype),
        grid=(m // tm,),
        in_specs=[pl.BlockSpec((tm, w), lambda i: (i, 0)), tab, tab],
        out_specs=pl.BlockSpec((tm, w), lambda i: (i, 0)),
        compiler_params=_cparams(("parallel",)),
        name="rope",
    )(x, cos, sin)


def _mla_attn_kernel(qn_ref, qp_ref, kn_ref, kp_ref, v_ref, o_ref, *, tq, scale, rope_dim):
    h = pl.program_id(1)
    t_len = qn_ref.shape[1]
    lane = _iota((tq, LANES), 1)
    mine = (lane // rope_dim) == (h % (LANES // rope_dim))
    for q0 in range(0, t_len, tq):
        kv_len = q0 + tq
        qn = qn_ref[0, q0:q0 + tq, :]
        qp = jnp.where(mine, qp_ref[0, q0:q0 + tq, :], 0.0)
        s = (_bdot_nt(qn, kn_ref[0, :kv_len, :]) + _bdot_nt(qp, kp_ref[0, :kv_len, :])) * scale
        ok = _iota((tq, kv_len), 1) <= _iota((tq, kv_len), 0) + q0
        s = jnp.where(ok, s, -jnp.inf)
        p = jnp.exp(s - jnp.max(s, axis=-1, keepdims=True))
        l = jnp.sum(p, axis=-1, keepdims=True)
        o_ref[0, q0:q0 + tq, :] = (_bdot(p, v_ref[0, :kv_len, :]) / l).astype(o_ref.dtype)


def mla_prompt_attention(qn, qp, kv, kp, rope_dim):
    b, t, w = qn.shape
    heads = w // ML_NOPE
    tq = min(512, t)
    scale = (ML_NOPE + rope_dim) ** -0.5
    per_tile = LANES // rope_dim
    slab = lambda f: pl.BlockSpec((1, t, LANES), f)
    return pl.pallas_call(
        functools.partial(_mla_attn_kernel, tq=tq, scale=scale, rope_dim=rope_dim),
        out_shape=jax.ShapeDtypeStruct((b, t, heads * ML_V), BF16),
        grid=(b, heads),
        in_specs=[slab(lambda bi, h: (bi, 0, h)), slab(lambda bi, h: (bi, 0, h // per_tile)),
                  slab(lambda bi, h: (bi, 0, 2 * h)), slab(lambda bi, h: (bi, 0, 0)),
                  slab(lambda bi, h: (bi, 0, 2 * h + 1))],
        out_specs=slab(lambda bi, h: (bi, 0, h)),
        compiler_params=_cparams(("parallel", "arbitrary")),
        name="mla_attn",
    )(qn, qp, kv, kp, kv)


def mla_project(u, pos, j, w_a, q_norm, kv_norm, w_qb):
    q_rank, kv_rank = q_norm.shape[-1], kv_norm.shape[-1]
    rope_dim = w_a.shape[-1] - q_rank - kv_rank
    a1 = mm(u, w_a, n=q_rank + kv_rank, tn=512, lead=(j,))
    kpe_raw = mm(u, _tail_cols(w_a, q_rank + kv_rank, j), n=LANES, tn=LANES)
    q_c = prenorm(a1, q_norm[j], width=q_rank)
    ckv = prenorm(a1, kv_norm[j], out_dtype=F32, col_block=q_rank // kv_rank, width=kv_rank)
    cos, sin = rope_tables(pos, rope_dim)
    k_pe = rope(kpe_raw, cos, sin, rope_dim, F32, dup=True)
    wq = w_qb[j].reshape(q_rank, ML_HEADS, ML_NOPE + rope_dim)
    w_nope = wq[:, :, :ML_NOPE].reshape(q_rank, ML_HEADS * ML_NOPE)
    w_pe = wq[:, :, ML_NOPE:].reshape(q_rank, ML_HEADS * rope_dim)
    q_nope = mm(q_c, w_nope, n=ML_HEADS * ML_NOPE, tn=512, out_dtype=BF16)
    q_pe = rope(mm(q_c, w_pe, n=ML_HEADS * rope_dim, tn=512), cos, sin, rope_dim, BF16)
    return q_nope, q_pe, ckv, k_pe, rope_dim


def mla_prompt_mixer(u, b, t, j, w_a, q_norm, kv_norm, w_qb, w_kvb, w_o):
    m, d = u.shape
    q_nope, q_pe, ckv, k_pe, rope_dim = mla_project(u, jnp.arange(t), j, w_a, q_norm, kv_norm, w_qb)
    kv = mm(ckv, w_kvb, n=w_kvb.shape[-1], tn=512, out_dtype=BF16, lead=(j,))
    o = mla_prompt_attention(q_nope.reshape(b, t, -1), q_pe.reshape(b, t, -1), kv.reshape(b, t, -1),
                             k_pe.astype(BF16).reshape(b, t, LANES), rope_dim)
    mix = mm(o.reshape(m, -1), w_o, n=d, tn=512, lead=(j,))
    rows = jnp.concatenate([ckv, k_pe[:, :rope_dim]], axis=-1).reshape(b, t, -1)
    return mix, rows


def _lane_pick(x, onehot):
    return jnp.sum(jnp.where(onehot, x, 0.0), axis=1, keepdims=True)


def conv_step(x, buf, w, b=None):
    xp = jnp.concatenate([buf, x[:, None, :]], axis=1)
    y = xp[:, 0] * w[0]
    for k in range(1, CONV_W):
        y = y + xp[:, k] * w[k]
    if b is not None:
        y = y + b
    return y, xp[:, 1:]


def _ssd_step_kernel(xs_ref, dt_ref, dtb_ref, alog_ref, dsk_ref, bm_ref, cm_ref, s_ref,
                     y_ref, so_ref, *, heads_per_group):
    heads = s_ref.shape[1]
    xs_t = xs_ref[0]
    dt = _softplus(dt_ref[0] + dtb_ref[...])
    dec = jnp.exp(dt * (-jnp.exp(alog_ref[...])))
    dtxs = xs_t * dt
    lane = _iota((1, heads), 1)
    cb = jnp.zeros((1, heads), F32)
    for g in range(bm_ref.shape[1]):
        cb_g = jnp.sum(cm_ref[0, g:g + 1, :] * bm_ref[0, g:g + 1, :], axis=1, keepdims=True)
        cb = jnp.where(lane // heads_per_group == g, cb_g, cb)

    def head(h, y_acc):
        s = s_ref[0, h]
        g = h // heads_per_group
        cm_row, bm_row = cm_ref[0, pl.ds(g, 1), :], bm_ref[0, pl.ds(g, 1), :]
        onehot = lane == h
        dec_h = _lane_pick(dec, onehot)
        y_col = jnp.sum(s * cm_row, axis=1, keepdims=True) * dec_h
        so_ref[0, h] = s * dec_h + _lane_pick(dtxs, onehot) * bm_row
        return jnp.where(onehot, y_col, y_acc)

    y_acc = lax.fori_loop(0, heads, head, jnp.zeros(xs_t.shape, F32))
    y_ref[0] = y_acc + cb * dtxs + xs_t * dsk_ref[...]


def ssd_step(xs, dt_raw, bm, cm, state, dt_bias, a_log, d_skip):
    b, heads, p, n = state.shape
    groups = bm.shape[1]
    xs_t = jnp.swapaxes(xs.reshape(b, heads, p), 1, 2)
    row = lambda a: a.reshape(1, heads)
    vec = pl.BlockSpec((1, heads), lambda i: (0, 0))
    y_t, new_state = pl.pallas_call(
        functools.partial(_ssd_step_kernel, heads_per_group=heads // groups),
        out_shape=(jax.ShapeDtypeStruct((b, p, heads), F32), jax.ShapeDtypeStruct(state.shape, F32)),
        grid=(b,),
        in_specs=[pl.BlockSpec((1, p, heads), lambda i: (i, 0, 0)),
                  pl.BlockSpec((1, 1, heads), lambda i: (i, 0, 0)), vec, vec, vec,
                  pl.BlockSpec((1, groups, n), lambda i: (i, 0, 0)),
                  pl.BlockSpec((1, groups, n), lambda i: (i, 0, 0)),
                  pl.BlockSpec((1, heads, p, n), lambda i: (i, 0, 0, 0))],
        out_specs=(pl.BlockSpec((1, p, heads), lambda i: (i, 0, 0)),
                   pl.BlockSpec((1, heads, p, n), lambda i: (i, 0, 0, 0))),
        compiler_params=_cparams(("parallel",)),
        name="ssd_step",
    )(xs_t, dt_raw.reshape(b, 1, heads), row(dt_bias), row(a_log), row(d_skip), bm, cm, state)
    return jnp.swapaxes(y_t, 1, 2).reshape(b, heads * p), new_state


def _gdn_step_kernel(q_ref, k_ref, v_ref, gate_ref, beta_ref, eg_ref, ng_ref, s_ref, o_ref, so_ref):
    heads = s_ref.shape[1]
    q_t, k_t = q_ref[0], k_ref[0]
    lane = _iota((1, heads), 1)

    def head(h, carry):
        s = s_ref[0, h]
        onehot = lane == h
        q_col, k_col = _lane_pick(q_t, onehot), _lane_pick(k_t, onehot)
        beta, eg = _lane_pick(beta_ref[0], onehot), _lane_pick(eg_ref[0], onehot)
        v_row = v_ref[0, pl.ds(h, 1), :]
        vn = v_row * beta - jnp.sum((k_col * (beta * eg)) * s, axis=0, keepdims=True)
        qk = jnp.sum(q_col * k_col, axis=0, keepdims=True)
        o = eg * jnp.sum(q_col * s, axis=0, keepdims=True) + qk * vn
        so_ref[0, h] = s * eg + k_col * vn
        o_ref[0, pl.ds(h, 1), :] = _rms(o, ng_ref[...]) * _silu(gate_ref[0, pl.ds(h, 1), :])
        return carry

    lax.fori_loop(0, heads, head, 0)


def gdn_step(q, k, v, gate, beta, eg, norm_g, state):
    b, heads, dk, dv = state.shape
    tr = lambda a: jnp.swapaxes(a, 1, 2)
    col = pl.BlockSpec((1, dk, heads), lambda i: (i, 0, 0))
    hrow = pl.BlockSpec((1, heads, dv), lambda i: (i, 0, 0))
    sc = pl.BlockSpec((1, 1, heads), lambda i: (i, 0, 0))
    st = pl.BlockSpec((1, heads, dk, dv), lambda i: (i, 0, 0, 0))
    return pl.pallas_call(
        _gdn_step_kernel,
        out_shape=(jax.ShapeDtypeStruct((b, heads, dv), F32), jax.ShapeDtypeStruct(state.shape, F32)),
        grid=(b,),
        in_specs=[col, col, hrow, hrow, sc, sc, pl.BlockSpec((1, dv), lambda i: (0, 0)), st],
        out_specs=(hrow, st),
        compiler_params=_cparams(("parallel",)),
        name="gdn_step",
    )(tr(q), tr(k), v, gate, beta.reshape(b, 1, heads), eg.reshape(b, 1, heads),
      norm_g.reshape(1, dv), state)


def _hgrn_step_kernel(q_ref, k_ref, e_ref, v_ref, gate_ref, ng_ref, s_ref, o_ref, so_ref):
    heads = s_ref.shape[1]
    q_t, k_t, e_t = q_ref[0], k_ref[0], e_ref[0]
    lane = _iota((1, heads), 1)

    def head(h, carry):
        s = s_ref[0, h]
        onehot = lane == h
        q_col, k_col, e_col = _lane_pick(q_t, onehot), _lane_pick(k_t, onehot), _lane_pick(e_t, onehot)
        v_row = v_ref[0, pl.ds(h, 1), :]
        qk = jnp.sum(q_col * k_col, axis=0, keepdims=True)
        o = qk * v_row + jnp.sum((q_col * e_col) * s, axis=0, keepdims=True)
        so_ref[0, h] = s * e_col + k_col * v_row
        o_ref[0, pl.ds(h, 1), :] = _rms(o, ng_ref[...]) * _silu(gate_ref[0, pl.ds(h, 1), :])
        return carry

    lax.fori_loop(0, heads, head, 0)


def hgrn_step(q, k, e, v, gate, norm_g, state):
    b, heads, dk, dv = state.shape
    tr = lambda a: jnp.swapaxes(a, 1, 2)
    col = pl.BlockSpec((1, dk, heads), lambda i: (i, 0, 0))
    hrow = pl.BlockSpec((1, heads, dv), lambda i: (i, 0, 0))
    st = pl.BlockSpec((1, heads, dk, dv), lambda i: (i, 0, 0, 0))
    return pl.pallas_call(
        _hgrn_step_kernel,
        out_shape=(jax.ShapeDtypeStruct((b, heads, dv), F32), jax.ShapeDtypeStruct(state.shape, F32)),
        grid=(b,),
        in_specs=[col, col, col, hrow, hrow, pl.BlockSpec((1, dv), lambda i: (0, 0)), st],
        out_specs=(hrow, st),
        compiler_params=_cparams(("parallel",)),
        name="hgrn_step",
    )(tr(q), tr(k), tr(e), v, gate, norm_g.reshape(1, dv), state)


def _head_in_kernel(x_ref, w_ref, o_ref):
    o_ref[0] = _bdot_nt(x_ref[...], w_ref[...])


def _head_out_kernel(x_ref, w_ref, o_ref):
    o_ref[...] = _bdot(x_ref[0], w_ref[...]).astype(o_ref.dtype)


def latent_in(q_nope, w_kvb, j):
    b = q_nope.shape[0]
    r = w_kvb.shape[-2]
    return pl.pallas_call(
        _head_in_kernel,
        out_shape=jax.ShapeDtypeStruct((ML_HEADS, b, r), F32),
        grid=(ML_HEADS,),
        in_specs=[pl.BlockSpec((b, ML_NOPE), lambda h: (0, h)),
                  pl.BlockSpec((None, r, ML_NOPE), lambda h: (j, 0, 2 * h))],
        out_specs=pl.BlockSpec((1, b, r), lambda h: (h, 0, 0)),
        compiler_params=_cparams(("parallel",)),
        name="mla_latent_in",
    )(q_nope, w_kvb)


def latent_out(o_lat, w_kvb, j):
    _, b, r = o_lat.shape
    return pl.pallas_call(
        _head_out_kernel,
        out_shape=jax.ShapeDtypeStruct((b, ML_HEADS * ML_V), BF16),
        grid=(ML_HEADS,),
        in_specs=[pl.BlockSpec((1, b, r), lambda h: (h, 0, 0)),
                  pl.BlockSpec((None, r, ML_V), lambda h: (j, 0, 2 * h + 1))],
        out_specs=pl.BlockSpec((b, ML_V), lambda h: (0, h)),
        compiler_params=_cparams(("parallel",)),
        name="mla_latent_out",
    )(o_lat, w_kvb)


PAGES_PER_STEP = 8


def _decode_kernel(pt_ref, q_ref, new_ref, *rest, scale, kv_rank):
    cache_refs = rest[:PAGES_PER_STEP]
    o_ref, m_s, l_s, acc_s = rest[PAGES_PER_STEP:]
    pg = pl.program_id(1)
    q = q_ref[0]

    @pl.when(pg == 0)
    def _():
        new = new_ref[0]
        m_s[...] = jnp.sum(q.astype(F32) * new.astype(BF16).astype(F32), axis=1, keepdims=True) * scale
        l_s[...] = jnp.ones_like(l_s)
        acc_s[...] = jnp.broadcast_to(new[:, :kv_rank], acc_s.shape)

    rows = [c[...].astype(BF16) for c in cache_refs]
    scores = [_bdot_nt(q, r) * scale for r in rows]
    m_old = m_s[...]
    m_new = m_old
    for s in scores:
        m_new = jnp.maximum(m_new, jnp.max(s, axis=1, keepdims=True))
    alpha = jnp.exp(m_old - m_new)
    l = l_s[...] * alpha
    acc = acc_s[...] * alpha
    for s, r in zip(scores, rows):
        p = jnp.exp(s - m_new)
        l = l + jnp.sum(p, axis=1, keepdims=True)
        acc = acc + _bdot(p, r[:, :kv_rank])
    m_s[...] = m_new
    l_s[...] = l
    acc_s[...] = acc

    @pl.when(pg == pl.num_programs(1) - 1)
    def _():
        o_ref[0] = acc / l


def mla_decode(q_cat, new_rows, cache, j, page_table, kv_rank):
    b, heads, row = q_cat.shape
    n_pages = page_table.shape[1]
    page = cache.shape[2]
    assert n_pages % PAGES_PER_STEP == 0
    scale = (ML_NOPE + row - kv_rank) ** -0.5

    def cache_spec(i):
        return pl.BlockSpec((None, None, page, row),
                            lambda bi, pg, pt: (j, pt[bi, pg * PAGES_PER_STEP + i], 0, 0))

    grid_spec = pltpu.PrefetchScalarGridSpec(
        num_scalar_prefetch=1,
        grid=(b, n_pages // PAGES_PER_STEP),
        in_specs=[pl.BlockSpec((1, heads, row), lambda bi, pg, pt: (bi, 0, 0)),
                  pl.BlockSpec((1, 1, row), lambda bi, pg, pt: (bi, 0, 0))]
                 + [cache_spec(i) for i in range(PAGES_PER_STEP)],
        out_specs=pl.BlockSpec((1, heads, kv_rank), lambda bi, pg, pt: (bi, 0, 0)),
        scratch_shapes=[pltpu.VMEM((heads, 1), F32), pltpu.VMEM((heads, 1), F32),
                        pltpu.VMEM((heads, kv_rank), F32)],
    )
    return pl.pallas_call(
        functools.partial(_decode_kernel, scale=scale, kv_rank=kv_rank),
        out_shape=jax.ShapeDtypeStruct((b, heads, kv_rank), F32),
        grid_spec=grid_spec,
        compiler_params=_cparams(("parallel", "arbitrary")),
        name="mla_decode",
    )(page_table, q_cat, new_rows, *([cache] * PAGES_PER_STEP))


def mamba_sample(u, j, conv_buf, state, w_in, conv_w, conv_b, dt_bias, a_log, d_skip, norm_g, w_out):
    b, d = u.shape
    d_inner = norm_g.shape[-1]
    heads = dt_bias.shape[-1]
    conv_dim = conv_w.shape[-1]
    n = state.shape[-1]
    z = mm(u, w_in, n=d_inner, tn=512, lead=(j,))
    xbc = mm(u, w_in, n=conv_dim, tn=512, col_block=d_inner // 512, lead=(j,))
    dt_raw = mm(u, w_in, n=heads, tn=LANES, col_block=(d_inner + conv_dim) // LANES, lead=(j,))
    xc, new_buf = conv_step(xbc, conv_buf, conv_w[j], conv_b[j])
    xc = _silu(xc)
    xs = xc[:, :d_inner]
    bm = xc[:, d_inner:d_inner + MB_GROUPS * n].reshape(b, MB_GROUPS, n)
    cm = xc[:, d_inner + MB_GROUPS * n:].reshape(b, MB_GROUPS, n)
    y, new_state = ssd_step(xs, dt_raw, bm, cm, state, dt_bias[j], a_log[j], d_skip[j])
    yn = prenorm(y * _silu(z), norm_g[j])
    return mm(yn, w_out, n=d, tn=256, lead=(j,)), new_state, new_buf


def gdn_sample(u, j, conv_buf, state, w_in, conv_w, dt_bias, a_log, norm_g, w_out):
    b, d = u.shape
    _, heads, dk, _ = state.shape
    width = heads * dk
    qkvg = mm(u, w_in, n=4 * width, tn=512, lead=(j,))
    ba = mm(u, _tail_cols(w_in, 4 * width, j), n=LANES, tn=LANES)
    qkv, new_buf = conv_step(qkvg[:, :3 * width], conv_buf, conv_w[j])
    qkv = _silu(qkv).reshape(b, 3, heads, dk)
    q = _l2norm(qkv[:, 0]) * (dk ** -0.5)
    k = _l2norm(qkv[:, 1])
    beta = jax.nn.sigmoid(ba[:, :heads])
    eg = jnp.exp(-jnp.exp(a_log[j]) * _softplus(ba[:, heads:2 * heads] + dt_bias[j]))
    gate = qkvg[:, 3 * width:].reshape(b, heads, dk)
    o, new_state = gdn_step(q, k, qkv[:, 2], gate, beta, eg, norm_g[j], state)
    return mm(o.reshape(b, width), w_out, n=d, tn=512, lead=(j,)), new_state, new_buf


def hgrn_sample(u, j, layer, state, w_in, lb_logits, norm_g, w_out):
    b, d = u.shape
    _, heads, dk, dv = state.shape
    qfig = mm(u, w_in, n=4 * d, tn=512, lead=(j,)).reshape(b, 4, heads, dk)
    lb_p = jax.nn.softmax(lb_logits, axis=0)
    lb = (jnp.cumsum(lb_p, axis=0) - lb_p[0])[layer].reshape(heads, dk)
    fg = lb + (1.0 - lb) * jax.nn.sigmoid(qfig[:, 1])
    q = _silu(qfig[:, 0]) * (dk ** -0.5)
    o, new_state = hgrn_step(q, 1.0 - fg, jnp.exp(jnp.log(fg)), qfig[:, 2], qfig[:, 3], norm_g[j], state)
    return mm(o.reshape(b, d), w_out, n=d, tn=512, lead=(j,)), new_state


def mla_sample(u, j, past_len, cache, page_table, w_a, q_norm, kv_norm, w_qb, w_kvb, w_o):
    b, d = u.shape
    kv_rank = kv_norm.shape[-1]
    pos = jnp.full((1,), past_len, jnp.int32)
    q_nope, q_pe, ckv, k_pe, rope_dim = mla_project(u, pos, j, w_a, q_norm, kv_norm, w_qb)
    new_rows = jnp.concatenate([ckv, k_pe[:, :rope_dim]], axis=-1)
    q_lat = jnp.swapaxes(latent_in(q_nope, w_kvb, j), 0, 1)
    q_cat = jnp.concatenate([q_lat.astype(BF16), q_pe.reshape(b, ML_HEADS, rope_dim)], axis=-1)
    o_lat = mla_decode(q_cat, new_rows[:, None, :], cache, j, page_table, kv_rank)
    o = latent_out(jnp.swapaxes(o_lat, 0, 1), w_kvb, j)
    return mm(o, w_o, n=d, tn=512, lead=(j,)), new_rows[:, None, :]


N_MIXERS = 4


def _trunk(x, b, t, st, p):
    norm_g = p["norm_g"]
    depth = norm_g.shape[0]
    outs = {k: [] for k in ("ssm", "ssm_conv", "gdn", "gdn_conv", "hgrn", "mla")}
    h = x
    xn = prenorm(h, norm_g[0, 0])
    for i in range(depth):
        mix_id, j = i % N_MIXERS, i // N_MIXERS
        g = norm_g[i]
        y = ffn(xn, p["ffn_w_gate"], p["ffn_w_up"], p["ffn_w_down"], i, 0)
        h, u = sandwich(h, y, g[1], g[2], 0.5)
        if mix_id == 0:
            args = (p["mb_w_in"], p["mb_conv_w"], p["mb_conv_b"], p["mb_dt_bias"], p["mb_a_log"],
                    p["mb_d"], p["mb_norm"], p["mb_w_out"])
            if st is None:
                mix, s1, c1 = mamba_prompt(u, b, t, j, *args)
            else:
                mix, s1, c1 = mamba_sample(u, j, st["ssm_conv"][j], st["ssm"][j], *args)
            outs["ssm"].append(s1)
            outs["ssm_conv"].append(c1)
        elif mix_id == 1:
            args = (p["gd_w_in"], p["gd_conv_w"], p["gd_dt_bias"], p["gd_a_log"], p["gd_norm"],
                    p["gd_w_out"])
            if st is None:
                mix, s1, c1 = gdn_prompt_mixer(u, b, t, j, *args)
            else:
                mix, s1, c1 = gdn_sample(u, j, st["gdn_conv"][j], st["gdn"][j], *args)
            outs["gdn"].append(s1)
            outs["gdn_conv"].append(c1)
        elif mix_id == 2:
            args = (p["hg_w_in"], p["hg_lb_logits"], p["hg_norm"], p["hg_w_out"])
            if st is None:
                mix, s1 = hgrn_prompt_mixer(u, b, t, j, i, *args)
            else:
                mix, s1 = hgrn_sample(u, j, i, st["hgrn"][j], *args)
            outs["hgrn"].append(s1)
        else:
            args = (p["ml_w_a"], p["ml_q_norm"], p["ml_kv_norm"], p["ml_w_qb"], p["ml_w_kvb"],
                    p["ml_w_o"])
            if st is None:
                mix, rows = mla_prompt_mixer(u, b, t, j, *args)
            else:
                mix, rows = mla_sample(u, j, st["past_len"], st["cache"], st["page_table"], *args)
            outs["mla"].append(rows)
        h, xn = sandwich(h, mix, g[3], g[4], 1.0)
        y = ffn(xn, p["ffn_w_gate"], p["ffn_w_up"], p["ffn_w_down"], i, 1)
        h, xn = sandwich(h, y, g[5], norm_g[i + 1, 0] if i + 1 < depth else None, 0.5)
    return (h.reshape(b, t, -1),) + tuple(jnp.stack(outs[k]) for k in
                                          ("ssm", "ssm_conv", "gdn", "gdn_conv", "hgrn", "mla"))


def kernel(x_prompt, x_sample, state_ssm, state_ssm_conv, state_gdn, state_gdn_conv, state_hgrn, cache_mla, page_table, norm_g, ffn_w_gate, ffn_w_up, ffn_w_down, mb_w_in, mb_conv_w, mb_conv_b, mb_dt_bias, mb_a_log, mb_d, mb_norm, mb_w_out, gd_w_in, gd_conv_w, gd_dt_bias, gd_a_log, gd_norm, gd_w_out, hg_w_in, hg_lb_logits, hg_norm, hg_w_out, ml_w_a, ml_q_norm, ml_kv_norm, ml_w_qb, ml_w_kvb, ml_w_o):
    p = dict(norm_g=norm_g, ffn_w_gate=ffn_w_gate, ffn_w_up=ffn_w_up, ffn_w_down=ffn_w_down,
             mb_w_in=mb_w_in, mb_conv_w=mb_conv_w, mb_conv_b=mb_conv_b, mb_dt_bias=mb_dt_bias,
             mb_a_log=mb_a_log, mb_d=mb_d, mb_norm=mb_norm, mb_w_out=mb_w_out,
             gd_w_in=gd_w_in, gd_conv_w=gd_conv_w, gd_dt_bias=gd_dt_bias, gd_a_log=gd_a_log,
             gd_norm=gd_norm, gd_w_out=gd_w_out,
             hg_w_in=hg_w_in, hg_lb_logits=hg_lb_logits, hg_norm=hg_norm, hg_w_out=hg_w_out,
             ml_w_a=ml_w_a, ml_q_norm=ml_q_norm, ml_kv_norm=ml_kv_norm, ml_w_qb=ml_w_qb,
             ml_w_kvb=ml_w_kvb, ml_w_o=ml_w_o)
    bp, tp, d = x_prompt.shape
    bs, ts, _ = x_sample.shape
    assert ts == 1
    prompt = _trunk(x_prompt.reshape(bp * tp, d), bp, tp, None, p)
    st = dict(ssm=state_ssm, ssm_conv=state_ssm_conv, gdn=state_gdn, gdn_conv=state_gdn_conv,
              hgrn=state_hgrn, cache=cache_mla, page_table=page_table,
              past_len=page_table.shape[1] * cache_mla.shape[2])
    sample = _trunk(x_sample.reshape(bs * ts, d), bs, ts, st, p)
    return (prompt[0], sample[0]) + prompt[1:] + sample[1:]
```

```python
import functools
import math

import jax
import jax.numpy as jnp
from jax import lax
from jax.experimental import pallas as pl
from jax.experimental.pallas import tpu as pltpu

F32 = jnp.float32
BF16 = jnp.bfloat16
HIGHEST = lax.Precision.HIGHEST

LANES = 128
VMEM_LIMIT_BYTES = 60 * 1024 * 1024

NORM_EPS = 1e-6
CONV_W = 4
CHUNK = 64
HG_CHUNK = 32
MB_HEADDIM = 64
MB_GROUPS = 8
ML_HEADS = 32
ML_NOPE = 128
ML_V = 128
ML_ROPE_THETA = 10000.0


def _cparams(sem):
    return pltpu.CompilerParams(dimension_semantics=sem, vmem_limit_bytes=VMEM_LIMIT_BYTES)


def _silu(x):
    return x * jax.nn.sigmoid(x)


def _softplus(x):
    return jnp.maximum(x, 0.0) + jnp.log1p(jnp.exp(-jnp.abs(x)))


def _bdot(a, b):
    return jnp.dot(a.astype(BF16), b.astype(BF16), preferred_element_type=F32)


def _bdot_nt(a, b):
    return lax.dot_general(a.astype(BF16), b.astype(BF16), (((1,), (1,)), ((), ())),
                           preferred_element_type=F32)


def _fdot(a, b):
    return jnp.dot(a, b, preferred_element_type=F32, precision=HIGHEST)


def _rms(x, g):
    return x * lax.rsqrt(jnp.mean(x * x, axis=-1, keepdims=True) + NORM_EPS) * g


def _prenorm_kernel(x_ref, g_ref, o_ref):
    o_ref[...] = _rms(x_ref[...].astype(F32), g_ref[...]).astype(o_ref.dtype)


def prenorm(x, g, out_dtype=BF16, col_block=0, width=None, tm=256):
    m = x.shape[0]
    d = width or x.shape[1]
    tm = min(tm, m)
    return pl.pallas_call(
        _prenorm_kernel,
        out_shape=jax.ShapeDtypeStruct((m, d), out_dtype),
        grid=(m // tm,),
        in_specs=[pl.BlockSpec((tm, d), lambda i: (i, col_block)),
                  pl.BlockSpec((1, d), lambda i: (0, 0))],
        out_specs=pl.BlockSpec((tm, d), lambda i: (i, 0)),
        compiler_params=_cparams(("parallel",)),
        name="prenorm",
    )(x, g.reshape(1, d))


def _sandwich_kernel(h_ref, y_ref, gpost_ref, gpre_ref, hn_ref, x_ref, *, scale):
    hn = h_ref[...] + scale * _rms(y_ref[...], gpost_ref[...])
    hn_ref[...] = hn
    x_ref[...] = _rms(hn, gpre_ref[...]).astype(x_ref.dtype)


def _sandwich_last_kernel(h_ref, y_ref, gpost_ref, hn_ref, *, scale):
    hn_ref[...] = h_ref[...] + scale * _rms(y_ref[...], gpost_ref[...])


def sandwich(h, y, g_post, g_pre, scale, tm=128):
    m, d = h.shape
    tm = min(tm, m)
    row = pl.BlockSpec((tm, d), lambda i: (i, 0))
    vec = pl.BlockSpec((1, d), lambda i: (0, 0))
    if g_pre is None:
        return pl.pallas_call(
            functools.partial(_sandwich_last_kernel, scale=scale),
            out_shape=jax.ShapeDtypeStruct((m, d), F32),
            grid=(m // tm,), in_specs=[row, row, vec], out_specs=row,
            compiler_params=_cparams(("parallel",)), name="sandwich_last",
        )(h, y, g_post.reshape(1, d)), None
    return pl.pallas_call(
        functools.partial(_sandwich_kernel, scale=scale),
        out_shape=(jax.ShapeDtypeStruct((m, d), F32), jax.ShapeDtypeStruct((m, d), BF16)),
        grid=(m // tm,), in_specs=[row, row, vec, vec], out_specs=(row, row),
        compiler_params=_cparams(("parallel",)), name="sandwich",
    )(h, y, g_post.reshape(1, d), g_pre.reshape(1, d))


def _ffn_rows(x_ref, wg_ref, wu_ref, wd_ref, o_ref, j, kc, nc):
    d = x_ref.shape[1]
    g = None
    u = None
    for k0 in range(0, d, kc):
        xk = x_ref[:, k0:k0 + kc]
        pg = jnp.dot(xk, wg_ref[k0:k0 + kc, :].astype(BF16), preferred_element_type=F32)
        pu = jnp.dot(xk, wu_ref[k0:k0 + kc, :].astype(BF16), preferred_element_type=F32)
        g = pg if g is None else g + pg
        u = pu if u is None else u + pu
    hmid = (_silu(g) * u).astype(BF16)

    @pl.when(j == 0)
    def _():
        o_ref[...] = jnp.zeros_like(o_ref)

    for n0 in range(0, d, nc):
        o_ref[:, n0:n0 + nc] += jnp.dot(hmid, wd_ref[:, n0:n0 + nc].astype(BF16),
                                        preferred_element_type=F32)


def _ffn_kernel(x_ref, xs_ref, wg_ref, wu_ref, wd_ref, o_ref, os_ref, *, kc, nc):
    i, j = pl.program_id(0), pl.program_id(1)
    _ffn_rows(x_ref, wg_ref, wu_ref, wd_ref, o_ref, j, kc, nc)

    @pl.when(i == 0)
    def _():
        _ffn_rows(xs_ref, wg_ref, wu_ref, wd_ref, os_ref, j, kc, nc)


def ffn(x, xs, wg, wu, wd, layer, which, tm=1024, tf=256):
    m, d = x.shape
    ms = xs.shape[0]
    f = wg.shape[-1]
    tm = min(tm, m)
    kc = min(1024, d)
    nc = min(512, d)
    once = dict(pipeline_mode=pl.Buffered(1))
    return pl.pallas_call(
        functools.partial(_ffn_kernel, kc=kc, nc=nc),
        out_shape=(jax.ShapeDtypeStruct((m, d), F32), jax.ShapeDtypeStruct((ms, d), F32)),
        grid=(m // tm, f // tf),
        in_specs=[
            pl.BlockSpec((tm, d), lambda i, j: (i, 0), **once),
            pl.BlockSpec((ms, d), lambda i, j: (0, 0), **once),
            pl.BlockSpec((None, None, d, tf), lambda i, j: (layer, which, 0, j)),
            pl.BlockSpec((None, None, d, tf), lambda i, j: (layer, which, 0, j)),
            pl.BlockSpec((None, None, tf, d), lambda i, j: (layer, which, j, 0)),
        ],
        out_specs=(pl.BlockSpec((tm, d), lambda i, j: (i, 0), **once),
                   pl.BlockSpec((ms, d), lambda i, j: (0, 0), **once)),
        compiler_params=_cparams(("arbitrary", "arbitrary")),
        name="ffn",
    )(x, xs, wg, wu, wd)


def _mm_kernel(x_ref, w_ref, o_ref, *, kc):
    k = x_ref.shape[1]
    acc = None
    for k0 in range(0, k, kc):
        p = jnp.dot(x_ref[:, k0:k0 + kc].astype(BF16), w_ref[k0:k0 + kc, :].astype(BF16),
                    preferred_element_type=F32)
        acc = p if acc is None else acc + p
    o_ref[...] = acc.astype(o_ref.dtype)


def mm(x, w, *, n, tn, col_block=0, tm=1024, out_dtype=F32, lead=None):
    m, k = x.shape
    tm = min(tm, m)
    lead = tuple(lead or ())
    assert w.ndim == 2 + len(lead) and w.shape[-2] == k and n % tn == 0 and m % tm == 0
    wspec = pl.BlockSpec((None,) * len(lead) + (k, tn), lambda i, j: lead + (0, j + col_block))
    return pl.pallas_call(
        functools.partial(_mm_kernel, kc=min(1024, k)),
        out_shape=jax.ShapeDtypeStruct((m, n), out_dtype),
        grid=(m // tm, n // tn),
        in_specs=[pl.BlockSpec((tm, k), lambda i, j: (i, 0)), wspec],
        out_specs=pl.BlockSpec((tm, tn), lambda i, j: (i, j)),
        compiler_params=_cparams(("parallel", "arbitrary")),
        name="mm",
    )(x, w)


def _iota(shape, dim):
    return lax.broadcasted_iota(jnp.int32, shape, dim)


def _conv_silu(x, w_ref, b_ref=None):
    row = _iota(x.shape, 0)
    acc = x * w_ref[CONV_W - 1:CONV_W, :]
    for k in range(1, CONV_W):
        shifted = jnp.where(row >= k, pltpu.roll(x, k, 0), 0.0)
        acc = acc + shifted * w_ref[CONV_W - 1 - k:CONV_W - k, :]
    if b_ref is not None:
        acc = acc + b_ref[...]
    return _silu(acc)


def _chunk_cumsum(x, chunk):
    pos = _iota(x.shape, 0) & (chunk - 1)
    k = 1
    while k < chunk:
        x = x + jnp.where(pos >= k, pltpu.roll(x, k, 0), 0.0)
        k *= 2
    return x


def _lane_select_matrix(sel):
    return (_iota((LANES, LANES), 0) == sel).astype(BF16)


def _split2(a):
    hi = a.astype(BF16)
    return hi, (a - hi.astype(F32)).astype(BF16)


def _expand(x, pick):
    h1 = x.astype(BF16)
    r1 = x - h1.astype(F32)
    h2 = r1.astype(BF16)
    h3 = (r1 - h2.astype(F32)).astype(BF16)
    dot = functools.partial(jnp.dot, preferred_element_type=F32)
    return dot(h1, pick) + dot(h2, pick) + dot(h3, pick)


def _dot3(a_parts, b_parts):
    (ah, al), (bh, bl) = a_parts, b_parts
    dot = functools.partial(jnp.dot, preferred_element_type=F32)
    return dot(ah, bh) + (dot(al, bh) + dot(ah, bl))


def _ssd_kernel(xs_ref, bm_ref, cm_ref, wx_ref, wb_ref, wc_ref, bx_ref, bb_ref, bc_ref,
                dt_ref, dtb_ref, alog_ref, dsk_ref, z_ref, y_ref, st_ref,
                xs_s, bm_s, cm_s, cum_s, dte_s, dt_s, la_s, *, pairs_per_group):
    hp = pl.program_id(1)
    t_len = xs_ref.shape[1]
    half = LANES // 2
    xs_s[...] = _conv_silu(xs_ref[0], wx_ref, bx_ref)

    @pl.when(hp % pairs_per_group == 0)
    def _():
        bm_s[...] = _conv_silu(bm_ref[0], wb_ref, bb_ref)
        cm_s[...] = _conv_silu(cm_ref[0], wc_ref, bc_ref)

    @pl.when(hp == 0)
    def _():
        dt = _softplus(dt_ref[0] + dtb_ref[...])
        dt_s[...] = dt
        la_s[...] = dt * (-jnp.exp(alog_ref[...]))

    lane_row = _iota((1, LANES), 1)
    pick = _lane_select_matrix(2 * hp + (lane_row >= half).astype(jnp.int32))
    dte_s[...] = _expand(dt_s[...], pick)
    cum_s[...] = _chunk_cumsum(_expand(la_s[...], pick), CHUNK)

    t_io = _iota((CHUNK, LANES), 0)
    s_io = _iota((CHUNK, LANES), 1) & (half - 1)
    causal = s_io <= t_io
    same_head = (_iota((LANES, LANES), 0) >= half) == (_iota((LANES, LANES), 1) >= half)
    low = lane_row < half
    zeros = jnp.zeros((CHUNK, LANES), F32)
    dskip = dsk_ref[...]

    blk = 2 * CHUNK
    n_blocks = t_len // blk
    group = 4 if n_blocks % 4 == 0 else 1

    def each(f, *lists):
        return [f(*args) for args in zip(*lists)]

    def row_form(m_t, cc):
        a0, a1 = m_t[0:1, :], m_t[half:half + 1, :]
        if cc == 0:
            return jnp.where(low, a0, pltpu.roll(a1, half, 1))
        return jnp.where(low, pltpu.roll(a0, half, 1), a1)

    def blocks(i, st):
        rows = [pl.ds(pl.multiple_of((group * i + b_i) * blk, blk), blk) for b_i in range(group)]
        x2, b2, c2, cu2, dt2 = ([ref[r, :] for r in rows] for ref in (xs_s, bm_s, cm_s, cum_s, dte_s))
        cu_t, dt_t, b_t = (each(lambda a: a.T, l) for l in (cu2, dt2, b2))
        ccs = [cc for _ in range(group) for cc in range(2)]
        sl = [slice(cc * CHUNK, (cc + 1) * CHUNK) for cc in ccs]
        per_chunk = lambda l: [l[n // 2][sl[n]] for n in range(2 * group)]
        xc, bc, cmc, ce, de = (per_chunk(l) for l in (x2, b2, c2, cu2, dt2))
        cum_row = [row_form(cu_t[n // 2], ccs[n]) for n in range(2 * group)]
        dt_row = [row_form(dt_t[n // 2], ccs[n]) for n in range(2 * group)]
        cb = each(lambda c, b: _bdot_nt(c, jnp.concatenate([b, b], axis=0)), cmc, bc)
        wp = each(lambda cb_, ce_, cr, dr: cb_ * jnp.exp(jnp.where(causal, ce_ - cr, -jnp.inf)) * dr,
                  cb, ce, cum_row, dt_row)
        bd = each(lambda x: jnp.where(same_head, jnp.concatenate([x, x], axis=0), 0.0), xc)
        y_intra = each(_bdot, wp, bd)
        cl = each(lambda ce_: ce_[CHUNK - 1:CHUNK, :], ce)
        tx = each(lambda cl_, ce_, de_, x: jnp.exp(cl_ - ce_) * de_ * x, cl, ce, de, xc)
        tx_pad = [jnp.concatenate([t_, zeros] if cc == 0 else [zeros, t_], axis=0)
                  for t_, cc in zip(tx, ccs)]
        upd = [_bdot(b_t[n // 2], tx_pad[n]) for n in range(2 * group)]
        dec = each(jnp.exp, cl)
        starts = []
        for n in range(2 * group):
            starts.append(st)
            st = st * dec[n] + upd[n]
        y_inter = each(_bdot, cmc, starts)
        y = each(lambda yi, ce_, yn, x: yi + jnp.exp(ce_) * yn + x * dskip, y_intra, ce, y_inter, xc)
        for b_i, r in enumerate(rows):
            y_ref[0, r, :] = jnp.concatenate(y[2 * b_i:2 * b_i + 2], axis=0) * _silu(z_ref[0, r, :])
        return st

    st = lax.fori_loop(0, n_blocks // group, blocks, jnp.zeros((LANES, LANES), F32))
    st_ref[0] = st.T


def ssd_prompt(xbc, dt_raw, z, conv_w, conv_b, dt_bias, a_log, d_skip):
    b, t, conv_dim = xbc.shape
    d_inner = z.shape[-1]
    heads = d_inner // MB_HEADDIM
    n = (conv_dim - d_inner) // (2 * MB_GROUPS)
    assert heads == LANES and n == LANES and t % (2 * CHUNK) == 0
    pairs = heads // 2
    ppg = pairs // MB_GROUPS
    xoff, boff = 0, d_inner // LANES
    coff = boff + MB_GROUPS

    def seq(off, grouped):
        if grouped:
            return lambda bi, hp: (bi, 0, off + hp // ppg)
        return lambda bi, hp: (bi, 0, off + hp)

    def par(off, grouped):
        if grouped:
            return lambda bi, hp: (0, off + hp // ppg)
        return lambda bi, hp: (0, off + hp)

    slab = lambda f: pl.BlockSpec((1, t, LANES), f)
    wsp = lambda f: pl.BlockSpec((CONV_W, LANES), f)
    vsp = lambda f: pl.BlockSpec((1, LANES), f)
    cb2 = conv_b.reshape(1, conv_dim)
    dsk = jnp.repeat(d_skip, MB_HEADDIM).reshape(1, d_inner)
    return pl.pallas_call(
        functools.partial(_ssd_kernel, pairs_per_group=ppg),
        out_shape=(jax.ShapeDtypeStruct((b, t, d_inner), F32),
                   jax.ShapeDtypeStruct((b, d_inner, n), F32)),
        grid=(b, pairs),
        in_specs=[slab(seq(xoff, False)), slab(seq(boff, True)), slab(seq(coff, True)),
                  wsp(par(xoff, False)), wsp(par(boff, True)), wsp(par(coff, True)),
                  vsp(par(xoff, False)), vsp(par(boff, True)), vsp(par(coff, True)),
                  slab(lambda bi, hp: (bi, 0, 0)), vsp(lambda bi, hp: (0, 0)), vsp(lambda bi, hp: (0, 0)),
                  vsp(lambda bi, hp: (0, hp)), slab(lambda bi, hp: (bi, 0, hp))],
        out_specs=(slab(lambda bi, hp: (bi, 0, hp)),
                   pl.BlockSpec((1, LANES, LANES), lambda bi, hp: (bi, hp, 0))),
        scratch_shapes=[pltpu.VMEM((t, LANES), F32)] * 7,
        compiler_params=_cparams(("arbitrary", "arbitrary")),
        name="ssd_prompt",
    )(xbc, xbc, xbc, conv_w, conv_w, conv_w, cb2, cb2, cb2,
      dt_raw, dt_bias.reshape(1, heads), a_log.reshape(1, heads), dsk, z)


def mamba_prompt(u, b, t, j, w_in, conv_w, conv_b, dt_bias, a_log, d_skip, norm_g, w_out):
    m, d = u.shape
    d_inner = norm_g.shape[-1]
    heads = dt_bias.shape[-1]
    conv_dim = conv_w.shape[-1]
    z = mm(u, w_in, n=d_inner, tn=512, lead=(j,))
    xbc = mm(u, w_in, n=conv_dim, tn=512, col_block=d_inner // 512, lead=(j,))
    dt_raw = mm(u, w_in, n=heads, tn=LANES, col_block=(d_inner + conv_dim) // LANES, lead=(j,))
    xbc3 = xbc.reshape(b, t, conv_dim)
    gated, state = ssd_prompt(xbc3, dt_raw.reshape(b, t, heads), z.reshape(b, t, d_inner),
                              conv_w[j], conv_b[j], dt_bias[j], a_log[j], d_skip[j])
    yn = prenorm(gated.reshape(m, d_inner), norm_g[j], tm=128)
    mix = mm(yn, w_out, n=d, tn=256, tm=1024, lead=(j,))
    n = state.shape[-1]
    return mix, state.reshape(b, heads, MB_HEADDIM, n), xbc3[:, t - (CONV_W - 1):, :]


def _l2norm(x):
    return x * lax.rsqrt(jnp.sum(x * x, axis=-1, keepdims=True) + NORM_EPS)


def _gdn_kernel(q_ref, k_ref, v_ref, gate_ref, wq_ref, wk_ref, wv_ref, ba_ref, dtb_ref, alog_ref,
                g_ref, o_ref, st_ref, q_s, k_s, v_s, beta_s, cum_s, u_s, w_s, a_s, *, heads):
    h = pl.program_id(1)
    t_len = q_ref.shape[1]
    dk = q_ref.shape[2]
    q_s[...] = _l2norm(_conv_silu(q_ref[0], wq_ref)) * (dk ** -0.5)
    k_s[...] = _l2norm(_conv_silu(k_ref[0], wk_ref))
    v_s[...] = _conv_silu(v_ref[0], wv_ref)
    ba = ba_ref[0]
    beta_all = jax.nn.sigmoid(ba)
    lg_all = -jnp.exp(alog_ref[...]) * _softplus(ba + dtb_ref[...])
    beta_s[...] = _expand(beta_all, _lane_select_matrix(h))
    cum_s[...] = _chunk_cumsum(_expand(lg_all, _lane_select_matrix(heads + h)), CHUNK)

    blk = 2 * CHUNK
    r_io, c_io = _iota((blk, blk), 0), _iota((blk, blk), 1)
    same_chunk = (r_io >= CHUNK) == (c_io >= CHUNK)
    strict = same_chunk & (c_io < r_io)
    incl = same_chunk & (c_io <= r_io)
    eye = (r_io == c_io).astype(F32)
    lane_row = _iota((1, blk), 1)
    zeros = jnp.zeros((CHUNK, dk), F32)

    n_blocks = t_len // blk
    group = 4 if n_blocks % 4 == 0 else 1

    def each(f, *lists):
        return [f(*args) for args in zip(*lists)]

    def solve_blocks(i, carry):
        rows = [pl.ds(pl.multiple_of((group * i + b_i) * blk, blk), blk) for b_i in range(group)]
        q2, k2, v2 = ([ref[r, :] for r in rows] for ref in (q_s, k_s, v_s))
        be2, g2 = ([ref[r, :] for r in rows] for ref in (beta_s, cum_s))
        diff = each(lambda g: g - g.T[0:1, :], g2)
        n1 = each(lambda be, k, d: -(be * _bdot_nt(k, k) * jnp.exp(jnp.where(strict, d, -jnp.inf))),
                  be2, k2, diff)
        square = lambda s: _dot3(s, s)
        pair = lambda a, b, sa, sb: eye + a + b + _dot3(sa, sb)
        s1 = each(_split2, n1)
        n2 = each(square, s1)
        s2 = each(_split2, n2)
        n4 = each(square, s2)
        f12 = each(pair, n1, n2, s1, s2)
        s4 = each(_split2, n4)
        n8 = each(square, s4)
        s8 = each(_split2, n8)
        n16 = each(square, s8)
        f48 = each(pair, n4, n8, s4, s8)
        s16 = each(_split2, n16)
        n32 = each(square, s16)
        f1632 = each(pair, n16, n32, s16, each(_split2, n32))
        f1248 = each(_dot3, each(_split2, f12), each(_split2, f48))
        t_inv = each(_dot3, each(_split2, f1248), each(_split2, f1632))
        rhs = each(lambda v, k, be, g: jnp.concatenate([v * be, k * (be * jnp.exp(g))], axis=1),
                   v2, k2, be2, g2)
        uw = each(_dot3, each(_split2, t_inv), each(_split2, rhs))
        attn = each(lambda q, k, d: _bdot_nt(q, k) * jnp.exp(jnp.where(incl, d, -jnp.inf)),
                    q2, k2, diff)
        for r, uw_b, a_b in zip(rows, uw, attn):
            u_s[r, :] = uw_b[:, :dk]
            w_s[r, :] = uw_b[:, dk:]
            a_s[r, :] = a_b
        return carry

    lax.fori_loop(0, n_blocks // group, solve_blocks, 0)

    def pair_of_chunks(i, st):
        rows = pl.ds(pl.multiple_of(i * blk, blk), blk)
        q2, k2, g2 = q_s[rows, :], k_s[rows, :], cum_s[rows, :]
        u2, w2 = u_s[rows, :], w_s[rows, :]
        g_row = g2.T[0:1, :]
        k_t = k2.T
        qe2 = q2 * jnp.exp(g2)
        vns, o_inter = [], []
        for cc in range(2):
            sl = slice(cc * CHUNK, (cc + 1) * CHUNK)
            vn = u2[sl] - _bdot(w2[sl], st)
            o_inter.append(_bdot(qe2[sl], st))
            g_last = g2[(cc + 1) * CHUNK - 1:(cc + 1) * CHUNK, :]
            in_chunk = (lane_row >= cc * CHUNK) & (lane_row < (cc + 1) * CHUNK)
            kd = k_t * jnp.where(in_chunk, jnp.exp(g_last - g_row), 0.0)
            vn_pad = jnp.concatenate([vn, zeros] if cc == 0 else [zeros, vn], axis=0)
            st = st * jnp.exp(g_last) + _bdot(kd, vn_pad)
            vns.append(vn)
        o2 = jnp.concatenate(o_inter, axis=0) + _bdot(a_s[rows, :], jnp.concatenate(vns, axis=0))
        o_ref[0, rows, :] = (_rms(o2, g_ref[...]) * _silu(gate_ref[0, rows, :])).astype(o_ref.dtype)
        return st

    st = lax.fori_loop(0, t_len // blk, pair_of_chunks, jnp.zeros((dk, dk), F32))
    st_ref[0, 0] = st


def gdn_prompt(qkvg, ba, conv_w, dt_bias, a_log, norm_g):
    b, t, w4 = qkvg.shape
    width = w4 // 4
    dk = norm_g.shape[-1]
    heads = width // dk
    assert dk == LANES and 2 * heads <= LANES and t % (2 * CHUNK) == 0
    pad = LANES - 2 * heads
    dtb = jnp.pad(dt_bias, (heads, pad)).reshape(1, LANES)
    alog = jnp.pad(a_log, (heads, pad)).reshape(1, LANES)
    slab = lambda off: pl.BlockSpec((1, t, dk), lambda bi, h: (bi, 0, off + h))
    wsp = lambda off: pl.BlockSpec((CONV_W, dk), lambda bi, h: (0, off + h))
    vec = pl.BlockSpec((1, LANES), lambda bi, h: (0, 0))
    return pl.pallas_call(
        functools.partial(_gdn_kernel, heads=heads),
        out_shape=(jax.ShapeDtypeStruct((b, t, width), BF16),
                   jax.ShapeDtypeStruct((b, heads, dk, dk), F32)),
        grid=(b, heads),
        in_specs=[slab(0), slab(heads), slab(2 * heads), slab(3 * heads),
                  wsp(0), wsp(heads), wsp(2 * heads),
                  pl.BlockSpec((1, t, LANES), lambda bi, h: (bi, 0, 0)), vec, vec, vec],
        out_specs=(pl.BlockSpec((1, t, dk), lambda bi, h: (bi, 0, h)),
                   pl.BlockSpec((1, 1, dk, dk), lambda bi, h: (bi, h, 0, 0))),
        scratch_shapes=[pltpu.VMEM((t, dk), F32)] * 8,
        compiler_params=_cparams(("parallel", "arbitrary")),
        name="gdn_prompt",
    )(qkvg, qkvg, qkvg, qkvg, conv_w, conv_w, conv_w, ba, dtb, alog, norm_g.reshape(1, dk))


def _tail_cols(w, start, lead):
    cols = w[lead][:, start:]
    return jnp.pad(cols, ((0, 0), (0, LANES - cols.shape[1])))


def gdn_prompt_mixer(u, b, t, j, w_in, conv_w, dt_bias, a_log, norm_g, w_out):
    m, d = u.shape
    width = conv_w.shape[-1] // 3
    qkvg = mm(u, w_in, n=4 * width, tn=512, lead=(j,))
    ba = mm(u, _tail_cols(w_in, 4 * width, j), n=LANES, tn=LANES)
    o, state = gdn_prompt(qkvg.reshape(b, t, 4 * width), ba.reshape(b, t, LANES), conv_w[j],
                          dt_bias[j], a_log[j], norm_g[j])
    mix = mm(o.reshape(m, width), w_out, n=d, tn=512, lead=(j,))
    new_buf = qkvg.reshape(b, t, 4 * width)[:, t - (CONV_W - 1):, :3 * width]
    return mix, state, new_buf


def _hgrn_lower_bound(lb_ref, layer):
    logits = lb_ref[...]
    e = jnp.exp(logits - jnp.max(logits, axis=0, keepdims=True))
    p = e / jnp.sum(e, axis=0, keepdims=True)
    lb = jnp.zeros((1, logits.shape[1]), F32)
    for r in range(1, layer + 1):
        lb = lb + p[r:r + 1, :]
    return lb


def _hgrn_kernel(q_ref, f_ref, i_ref, g_ref, lb_ref, ng_ref, o_ref, st_ref,
                 q_s, k_s, cum_s, *, layer):
    t_len = q_ref.shape[1]
    dk = q_ref.shape[2]
    lb = _hgrn_lower_bound(lb_ref, layer)
    fg = lb + (1.0 - lb) * jax.nn.sigmoid(f_ref[0])
    q_s[...] = _silu(q_ref[0]) * (dk ** -0.5)
    k_s[...] = 1.0 - fg
    cum_s[...] = _chunk_cumsum(jnp.log(fg), HG_CHUNK)

    blk = 4 * HG_CHUNK
    t_io = _iota((HG_CHUNK, dk), 0)
    lane = _iota((HG_CHUNK, blk), 1)
    row_blk = _iota((blk, dk), 0)

    def block_of_chunks(i, st_t):
        r0 = pl.multiple_of(i * blk, blk)
        rows = pl.ds(r0, blk)
        v4 = i_ref[0, rows, :]
        v4_t = v4.T
        outs = []
        for cc in range(4):
            c0 = cc * HG_CHUNK
            crow = pl.ds(r0 + c0, HG_CHUNK)
            qc, kc, cu = q_s[crow, :], k_s[crow, :], cum_s[crow, :]
            attn = jnp.zeros((HG_CHUNK, blk), F32)
            for s in range(HG_CHUNK):
                srow = pl.ds(r0 + c0 + s, 1)
                decay = jnp.exp(jnp.where(t_io >= s, cu - cum_s[srow, :], -jnp.inf))
                col = jnp.sum(qc * k_s[srow, :] * decay, axis=1, keepdims=True)
                attn = jnp.where(lane == c0 + s, col, attn)
            o = _bdot(attn, v4) + _bdot_nt(qc * jnp.exp(cu), st_t)
            outs.append(o)
            last = cu[HG_CHUNK - 1:HG_CHUNK, :]
            in_chunk = (row_blk >= c0) & (row_blk < c0 + HG_CHUNK)
            kd = jnp.where(in_chunk, k_s[rows, :] * jnp.exp(last - cum_s[rows, :]), 0.0)
            st_t = st_t * jnp.exp(last) + _bdot(v4_t, kd)
        o4 = jnp.concatenate(outs, axis=0)
        o_ref[0, rows, :] = (_rms(o4, ng_ref[...]) * _silu(g_ref[0, rows, :])).astype(o_ref.dtype)
        return st_t

    st_t = lax.fori_loop(0, t_len // blk, block_of_chunks, jnp.zeros((dk, dk), F32))
    st_ref[0, 0] = st_t.T


def hgrn_prompt(qfig, lb_logits, norm_g, layer):
    b, t, d4 = qfig.shape
    d = d4 // 4
    dk = norm_g.shape[-1]
    heads = d // dk
    depth = lb_logits.shape[0]
    assert dk == LANES and t % (4 * HG_CHUNK) == 0
    slab = lambda off: pl.BlockSpec((1, t, dk), lambda bi, h: (bi, 0, off + h))
    return pl.pallas_call(
        functools.partial(_hgrn_kernel, layer=layer),
        out_shape=(jax.ShapeDtypeStruct((b, t, d), BF16),
                   jax.ShapeDtypeStruct((b, heads, dk, dk), F32)),
        grid=(b, heads),
        in_specs=[slab(0), slab(heads), slab(2 * heads), slab(3 * heads),
                  pl.BlockSpec((depth, dk), lambda bi, h: (0, h)),
                  pl.BlockSpec((1, dk), lambda bi, h: (0, 0))],
        out_specs=(pl.BlockSpec((1, t, dk), lambda bi, h: (bi, 0, h)),
                   pl.BlockSpec((1, 1, dk, dk), lambda bi, h: (bi, h, 0, 0))),
        scratch_shapes=[pltpu.VMEM((t, dk), F32)] * 3,
        compiler_params=_cparams(("parallel", "arbitrary")),
        name="hgrn_prompt",
    )(qfig, qfig, qfig, qfig, lb_logits, norm_g.reshape(1, dk))


def hgrn_prompt_mixer(u, b, t, j, layer, w_in, lb_logits, norm_g, w_out):
    m, d = u.shape
    qfig = mm(u, w_in, n=4 * d, tn=512, lead=(j,))
    o, state = hgrn_prompt(qfig.reshape(b, t, 4 * d), lb_logits, norm_g[j], layer)
    mix = mm(o.reshape(m, d), w_out, n=d, tn=512, lead=(j,))
    return mix, state


def rope_tables(pos, rope_dim):
    half = rope_dim // 2
    inv = ML_ROPE_THETA ** (-jnp.arange(half, dtype=F32) / half)
    ang = pos.astype(F32)[:, None] * inv
    cos, sin = jnp.cos(ang), jnp.sin(ang)
    reps = LANES // rope_dim
    return (jnp.tile(jnp.concatenate([cos, cos], axis=-1), (1, reps)),
            jnp.tile(jnp.concatenate([-sin, sin], axis=-1), (1, reps)))


def _rope_kernel(x_ref, cos_ref, sin_ref, o_ref, *, rope_dim, dup):
    half = rope_dim // 2
    lane = _iota((x_ref.shape[0], LANES), 1)
    first = (lane & (rope_dim - 1)) < half
    cos, sin = cos_ref[...], sin_ref[...]
    for c in range(x_ref.shape[1] // LANES):
        x = x_ref[:, c * LANES:(c + 1) * LANES]
        other = jnp.where(first, pltpu.roll(x, LANES - half, 1), pltpu.roll(x, half, 1))
        y = x * cos + other * sin
        if dup:
            y = y + pltpu.roll(y, rope_dim, 1)
        o_ref[:, c * LANES:(c + 1) * LANES] = y.astype(o_ref.dtype)


def rope(x, cos, sin, rope_dim, out_dtype, dup=False, tm=256):
    m, w = x.shape
    t = cos.shape[0]
    if t == 1:
        tm = min(tm, m)
        tab = pl.BlockSpec((1, LANES), lambda i: (0, 0))
    else:
        tm = min(tm, t)
        nt = t // tm
        tab = pl.BlockSpec((tm, LANES), lambda i: (i % nt, 0))
    return pl.pallas_call(
        functools.partial(_rope_kernel, rope_dim=rope_dim, dup=dup),
        out_shape=jax.ShapeDtypeStruct((m, w), out_dtype),
        grid=(m // tm,),
        in_specs=[pl.BlockSpec((tm, w), lambda i: (i, 0)), tab, tab],
        out_specs=pl.BlockSpec((tm, w), lambda i: (i, 0)),
        compiler_params=_cparams(("parallel",)),
        name="rope",
    )(x, cos, sin)


def _mla_attn_kernel(qn_ref, qp_ref, kn_ref, kp_ref, v_ref, o_ref, *, tq, scale, rope_dim):
    h = pl.program_id(1)
    t_len = qn_ref.shape[1]
    lane = _iota((tq, LANES), 1)
    mine = (lane // rope_dim) == (h % (LANES // rope_dim))
    for q0 in range(0, t_len, tq):
        kv_len = q0 + tq
        qn = qn_ref[0, q0:q0 + tq, :]
        qp = jnp.where(mine, qp_ref[0, q0:q0 + tq, :], 0.0)
        s = (_bdot_nt(qn, kn_ref[0, :kv_len, :]) + _bdot_nt(qp, kp_ref[0, :kv_len, :])) * scale
        ok = _iota((tq, kv_len), 1) <= _iota((tq, kv_len), 0) + q0
        s = jnp.where(ok, s, -jnp.inf)
        p = jnp.exp(s - jnp.max(s, axis=-1, keepdims=True))
        l = jnp.sum(p, axis=-1, keepdims=True)
        o_ref[0, q0:q0 + tq, :] = (_bdot(p, v_ref[0, :kv_len, :]) / l).astype(o_ref.dtype)


def mla_prompt_attention(qn, qp, kv, kp, rope_dim):
    b, t, w = qn.shape
    heads = w // ML_NOPE
    tq = min(512, t)
    scale = (ML_NOPE + rope_dim) ** -0.5
    per_tile = LANES // rope_dim
    slab = lambda f: pl.BlockSpec((1, t, LANES), f)
    return pl.pallas_call(
        functools.partial(_mla_attn_kernel, tq=tq, scale=scale, rope_dim=rope_dim),
        out_shape=jax.ShapeDtypeStruct((b, t, heads * ML_V), BF16),
        grid=(b, heads),
        in_specs=[slab(lambda bi, h: (bi, 0, h)), slab(lambda bi, h: (bi, 0, h // per_tile)),
                  slab(lambda bi, h: (bi, 0, 2 * h)), slab(lambda bi, h: (bi, 0, 0)),
                  slab(lambda bi, h: (bi, 0, 2 * h + 1))],
        out_specs=slab(lambda bi, h: (bi, 0, h)),
        compiler_params=_cparams(("parallel", "arbitrary")),
        name="mla_attn",
    )(qn, qp, kv, kp, kv)


def mla_project(u, pos, j, w_a, q_norm, kv_norm, w_qb):
    q_rank, kv_rank = q_norm.shape[-1], kv_norm.shape[-1]
    rope_dim = w_a.shape[-1] - q_rank - kv_rank
    a1 = mm(u, w_a, n=q_rank + kv_rank, tn=512, lead=(j,))
    kpe_raw = mm(u, _tail_cols(w_a, q_rank + kv_rank, j), n=LANES, tn=LANES)
    q_c = prenorm(a1, q_norm[j], width=q_rank)
    ckv = prenorm(a1, kv_norm[j], out_dtype=F32, col_block=q_rank // kv_rank, width=kv_rank)
    cos, sin = rope_tables(pos, rope_dim)
    k_pe = rope(kpe_raw, cos, sin, rope_dim, F32, dup=True)
    wq = w_qb[j].reshape(q_rank, ML_HEADS, ML_NOPE + rope_dim)
    w_nope = wq[:, :, :ML_NOPE].reshape(q_rank, ML_HEADS * ML_NOPE)
    w_pe = wq[:, :, ML_NOPE:].reshape(q_rank, ML_HEADS * rope_dim)
    q_nope = mm(q_c, w_nope, n=ML_HEADS * ML_NOPE, tn=512, out_dtype=BF16)
    q_pe = rope(mm(q_c, w_pe, n=ML_HEADS * rope_dim, tn=512), cos, sin, rope_dim, BF16)
    return q_nope, q_pe, ckv, k_pe, rope_dim


def mla_prompt_mixer(u, b, t, j, w_a, q_norm, kv_norm, w_qb, w_kvb, w_o):
    m, d = u.shape
    q_nope, q_pe, ckv, k_pe, rope_dim = mla_project(u, jnp.arange(t), j, w_a, q_norm, kv_norm, w_qb)
    kv = mm(ckv, w_kvb, n=w_kvb.shape[-1], tn=512, out_dtype=BF16, lead=(j,))
    o = mla_prompt_attention(q_nope.reshape(b, t, -1), q_pe.reshape(b, t, -1), kv.reshape(b, t, -1),
                             k_pe.astype(BF16).reshape(b, t, LANES), rope_dim)
    mix = mm(o.reshape(m, -1), w_o, n=d, tn=512, lead=(j,))
    rows = jnp.concatenate([ckv, k_pe[:, :rope_dim]], axis=-1).reshape(b, t, -1)
    return mix, rows


def _lane_pick(x, onehot):
    return jnp.sum(jnp.where(onehot, x, 0.0), axis=1, keepdims=True)


_HEADS_PER_ITER_SSD = 8
_HEADS_PER_ITER = 4


def conv_step(x, buf, w, b=None):
    xp = jnp.concatenate([buf, x[:, None, :]], axis=1)
    y = xp[:, 0] * w[0]
    for k in range(1, CONV_W):
        y = y + xp[:, k] * w[k]
    if b is not None:
        y = y + b
    return y, xp[:, 1:]


def _ssd_step_kernel(xs_ref, dt_ref, dtb_ref, alog_ref, dsk_ref, bm_ref, cm_ref, s_ref,
                     y_ref, so_ref, *, heads_per_group):
    heads = s_ref.shape[1]
    xs_t = xs_ref[0]
    dt = _softplus(dt_ref[0] + dtb_ref[...])
    dec = jnp.exp(dt * (-jnp.exp(alog_ref[...])))
    dtxs = xs_t * dt
    lane = _iota((1, heads), 1)
    cb = jnp.zeros((1, heads), F32)
    for g in range(bm_ref.shape[1]):
        cb_g = jnp.sum(cm_ref[0, g:g + 1, :] * bm_ref[0, g:g + 1, :], axis=1, keepdims=True)
        cb = jnp.where(lane // heads_per_group == g, cb_g, cb)

    hb = _HEADS_PER_ITER_SSD
    assert heads_per_group % hb == 0

    def head_block(i, y_acc):
        hs = [i * hb + k for k in range(hb)]
        g = (i * hb) // heads_per_group
        cm_row, bm_row = cm_ref[0, pl.ds(g, 1), :], bm_ref[0, pl.ds(g, 1), :]
        s = [s_ref[0, h] for h in hs]
        onehot = [lane == h for h in hs]
        dec_h = [_lane_pick(dec, o) for o in onehot]
        dx_col = [_lane_pick(dtxs, o) for o in onehot]
        y_col = [jnp.sum(s_ * cm_row, axis=1, keepdims=True) * d for s_, d in zip(s, dec_h)]
        for h, s_, d, x in zip(hs, s, dec_h, dx_col):
            so_ref[0, h] = s_ * d + x * bm_row
        for o, y in zip(onehot, y_col):
            y_acc = jnp.where(o, y, y_acc)
        return y_acc

    y_acc = lax.fori_loop(0, heads // hb, head_block, jnp.zeros(xs_t.shape, F32))
    y_ref[0] = y_acc + cb * dtxs + xs_t * dsk_ref[...]


def ssd_step(xs, dt_raw, bm, cm, state, dt_bias, a_log, d_skip):
    b, heads, p, n = state.shape
    groups = bm.shape[1]
    xs_t = jnp.swapaxes(xs.reshape(b, heads, p), 1, 2)
    row = lambda a: a.reshape(1, heads)
    vec = pl.BlockSpec((1, heads), lambda i: (0, 0))
    y_t, new_state = pl.pallas_call(
        functools.partial(_ssd_step_kernel, heads_per_group=heads // groups),
        out_shape=(jax.ShapeDtypeStruct((b, p, heads), F32), jax.ShapeDtypeStruct(state.shape, F32)),
        grid=(b,),
        in_specs=[pl.BlockSpec((1, p, heads), lambda i: (i, 0, 0)),
                  pl.BlockSpec((1, 1, heads), lambda i: (i, 0, 0)), vec, vec, vec,
                  pl.BlockSpec((1, groups, n), lambda i: (i, 0, 0)),
                  pl.BlockSpec((1, groups, n), lambda i: (i, 0, 0)),
                  pl.BlockSpec((1, heads, p, n), lambda i: (i, 0, 0, 0))],
        out_specs=(pl.BlockSpec((1, p, heads), lambda i: (i, 0, 0)),
                   pl.BlockSpec((1, heads, p, n), lambda i: (i, 0, 0, 0))),
        compiler_params=_cparams(("parallel",)),
        name="ssd_step",
    )(xs_t, dt_raw.reshape(b, 1, heads), row(dt_bias), row(a_log), row(d_skip), bm, cm, state)
    return jnp.swapaxes(y_t, 1, 2).reshape(b, heads * p), new_state


def _gdn_step_kernel(q_ref, k_ref, v_ref, gate_ref, beta_ref, eg_ref, ng_ref, s_ref, o_ref, so_ref):
    heads = s_ref.shape[1]
    q_t, k_t = q_ref[0], k_ref[0]
    lane = _iota((1, heads), 1)

    hb = _HEADS_PER_ITER
    col_sum = lambda x: jnp.sum(x, axis=0, keepdims=True)

    def head_block(i, carry):
        hs = [i * hb + k for k in range(hb)]
        s = [s_ref[0, h] for h in hs]
        onehot = [lane == h for h in hs]
        q_col = [_lane_pick(q_t, o) for o in onehot]
        k_col = [_lane_pick(k_t, o) for o in onehot]
        beta = [_lane_pick(beta_ref[0], o) for o in onehot]
        eg = [_lane_pick(eg_ref[0], o) for o in onehot]
        v_row = [v_ref[0, pl.ds(h, 1), :] for h in hs]
        vn = [v * b - col_sum((k * (b * e)) * s_) for v, b, e, k, s_ in zip(v_row, beta, eg, k_col, s)]
        qk = [col_sum(q * k) for q, k in zip(q_col, k_col)]
        o = [e * col_sum(q * s_) + a * n for e, q, s_, a, n in zip(eg, q_col, s, qk, vn)]
        for h, s_, e, k, n, o_h in zip(hs, s, eg, k_col, vn, o):
            so_ref[0, h] = s_ * e + k * n
            o_ref[0, pl.ds(h, 1), :] = _rms(o_h, ng_ref[...]) * _silu(gate_ref[0, pl.ds(h, 1), :])
        return carry

    lax.fori_loop(0, heads // hb, head_block, 0)


def gdn_step(q, k, v, gate, beta, eg, norm_g, state):
    b, heads, dk, dv = state.shape
    tr = lambda a: jnp.swapaxes(a, 1, 2)
    col = pl.BlockSpec((1, dk, heads), lambda i: (i, 0, 0))
    hrow = pl.BlockSpec((1, heads, dv), lambda i: (i, 0, 0))
    sc = pl.BlockSpec((1, 1, heads), lambda i: (i, 0, 0))
    st = pl.BlockSpec((1, heads, dk, dv), lambda i: (i, 0, 0, 0))
    return pl.pallas_call(
        _gdn_step_kernel,
        out_shape=(jax.ShapeDtypeStruct((b, heads, dv), F32), jax.ShapeDtypeStruct(state.shape, F32)),
        grid=(b,),
        in_specs=[col, col, hrow, hrow, sc, sc, pl.BlockSpec((1, dv), lambda i: (0, 0)), st],
        out_specs=(hrow, st),
        compiler_params=_cparams(("parallel",)),
        name="gdn_step",
    )(tr(q), tr(k), v, gate, beta.reshape(b, 1, heads), eg.reshape(b, 1, heads),
      norm_g.reshape(1, dv), state)


def _hgrn_step_kernel(q_ref, k_ref, e_ref, v_ref, gate_ref, ng_ref, s_ref, o_ref, so_ref):
    heads = s_ref.shape[1]
    q_t, k_t, e_t = q_ref[0], k_ref[0], e_ref[0]
    lane = _iota((1, heads), 1)

    hb = _HEADS_PER_ITER
    col_sum = lambda x: jnp.sum(x, axis=0, keepdims=True)

    def head_block(i, carry):
        hs = [i * hb + k for k in range(hb)]
        s = [s_ref[0, h] for h in hs]
        onehot = [lane == h for h in hs]
        q_col = [_lane_pick(q_t, o) for o in onehot]
        k_col = [_lane_pick(k_t, o) for o in onehot]
        e_col = [_lane_pick(e_t, o) for o in onehot]
        v_row = [v_ref[0, pl.ds(h, 1), :] for h in hs]
        qk = [col_sum(q * k) for q, k in zip(q_col, k_col)]
        o = [a * v + col_sum((q * e) * s_) for a, v, q, e, s_ in zip(qk, v_row, q_col, e_col, s)]
        for h, s_, e, k, v, o_h in zip(hs, s, e_col, k_col, v_row, o):
            so_ref[0, h] = s_ * e + k * v
            o_ref[0, pl.ds(h, 1), :] = _rms(o_h, ng_ref[...]) * _silu(gate_ref[0, pl.ds(h, 1), :])
        return carry

    lax.fori_loop(0, heads // hb, head_block, 0)


def hgrn_step(q, k, e, v, gate, norm_g, state):
    b, heads, dk, dv = state.shape
    tr = lambda a: jnp.swapaxes(a, 1, 2)
    col = pl.BlockSpec((1, dk, heads), lambda i: (i, 0, 0))
    hrow = pl.BlockSpec((1, heads, dv), lambda i: (i, 0, 0))
    st = pl.BlockSpec((1, heads, dk, dv), lambda i: (i, 0, 0, 0))
    return pl.pallas_call(
        _hgrn_step_kernel,
        out_shape=(jax.ShapeDtypeStruct((b, heads, dv), F32), jax.ShapeDtypeStruct(state.shape, F32)),
        grid=(b,),
        in_specs=[col, col, col, hrow, hrow, pl.BlockSpec((1, dv), lambda i: (0, 0)), st],
        out_specs=(hrow, st),
        compiler_params=_cparams(("parallel",)),
        name="hgrn_step",
    )(tr(q), tr(k), tr(e), v, gate, norm_g.reshape(1, dv), state)


def _head_in_kernel(x_ref, w_ref, o_ref):
    o_ref[0] = _bdot_nt(x_ref[...], w_ref[...])


def _head_out_kernel(x_ref, w_ref, o_ref):
    o_ref[...] = _bdot(x_ref[0], w_ref[...]).astype(o_ref.dtype)


def latent_in(q_nope, w_kvb, j):
    b = q_nope.shape[0]
    r = w_kvb.shape[-2]
    return pl.pallas_call(
        _head_in_kernel,
        out_shape=jax.ShapeDtypeStruct((ML_HEADS, b, r), F32),
        grid=(ML_HEADS,),
        in_specs=[pl.BlockSpec((b, ML_NOPE), lambda h: (0, h)),
                  pl.BlockSpec((None, r, ML_NOPE), lambda h: (j, 0, 2 * h))],
        out_specs=pl.BlockSpec((1, b, r), lambda h: (h, 0, 0)),
        compiler_params=_cparams(("parallel",)),
        name="mla_latent_in",
    )(q_nope, w_kvb)


def latent_out(o_lat, w_kvb, j):
    _, b, r = o_lat.shape
    return pl.pallas_call(
        _head_out_kernel,
        out_shape=jax.ShapeDtypeStruct((b, ML_HEADS * ML_V), BF16),
        grid=(ML_HEADS,),
        in_specs=[pl.BlockSpec((1, b, r), lambda h: (h, 0, 0)),
                  pl.BlockSpec((None, r, ML_V), lambda h: (j, 0, 2 * h + 1))],
        out_specs=pl.BlockSpec((b, ML_V), lambda h: (0, h)),
        compiler_params=_cparams(("parallel",)),
        name="mla_latent_out",
    )(o_lat, w_kvb)


PAGES_PER_STEP = 16


def _decode_kernel(pt_ref, q_ref, new_ref, *rest, scale, kv_rank):
    cache_refs = rest[:PAGES_PER_STEP]
    o_ref, m_s, l_s, acc_s = rest[PAGES_PER_STEP:]
    pg = pl.program_id(1)
    q = q_ref[0]

    @pl.when(pg == 0)
    def _():
        new = new_ref[0]
        m_s[...] = jnp.sum(q.astype(F32) * new.astype(BF16).astype(F32), axis=1, keepdims=True) * scale
        l_s[...] = jnp.ones_like(l_s)
        acc_s[...] = jnp.broadcast_to(new[:, :kv_rank], acc_s.shape)

    rows = [c[...].astype(BF16) for c in cache_refs]
    scores = [_bdot(q, r) * scale for r in rows]
    m_old = m_s[...]
    m_new = m_old
    for s in scores:
        m_new = jnp.maximum(m_new, jnp.max(s, axis=1, keepdims=True))
    alpha = jnp.exp(m_old - m_new)
    l = l_s[...] * alpha
    acc = acc_s[...] * alpha
    for s, r in zip(scores, rows):
        p = jnp.exp(s - m_new)
        l = l + jnp.sum(p, axis=1, keepdims=True)
        acc = acc + _bdot_nt(p, r[:kv_rank, :])
    m_s[...] = m_new
    l_s[...] = l
    acc_s[...] = acc

    @pl.when(pg == pl.num_programs(1) - 1)
    def _():
        o_ref[0] = acc / l


def mla_decode(q_cat, new_rows, cache, j, page_table, kv_rank):
    b, heads, row = q_cat.shape
    n_pages = page_table.shape[1]
    page = cache.shape[3]
    assert n_pages % PAGES_PER_STEP == 0
    scale = (ML_NOPE + row - kv_rank) ** -0.5

    def cache_spec(i):
        return pl.BlockSpec((None, None, row, page),
                            lambda bi, pg, pt: (j, pt[bi, pg * PAGES_PER_STEP + i], 0, 0))

    grid_spec = pltpu.PrefetchScalarGridSpec(
        num_scalar_prefetch=1,
        grid=(b, n_pages // PAGES_PER_STEP),
        in_specs=[pl.BlockSpec((1, heads, row), lambda bi, pg, pt: (bi, 0, 0)),
                  pl.BlockSpec((1, 1, row), lambda bi, pg, pt: (bi, 0, 0))]
                 + [cache_spec(i) for i in range(PAGES_PER_STEP)],
        out_specs=pl.BlockSpec((1, heads, kv_rank), lambda bi, pg, pt: (bi, 0, 0)),
        scratch_shapes=[pltpu.VMEM((heads, 1), F32), pltpu.VMEM((heads, 1), F32),
                        pltpu.VMEM((heads, kv_rank), F32)],
    )
    return pl.pallas_call(
        functools.partial(_decode_kernel, scale=scale, kv_rank=kv_rank),
        out_shape=jax.ShapeDtypeStruct((b, heads, kv_rank), F32),
        grid_spec=grid_spec,
        compiler_params=_cparams(("parallel", "arbitrary")),
        name="mla_decode",
    )(page_table, q_cat, new_rows, *([cache] * PAGES_PER_STEP))


def mamba_sample(u, j, conv_buf, state, w_in, conv_w, conv_b, dt_bias, a_log, d_skip, norm_g, w_out):
    b, d = u.shape
    d_inner = norm_g.shape[-1]
    heads = dt_bias.shape[-1]
    conv_dim = conv_w.shape[-1]
    n = state.shape[-1]
    z = mm(u, w_in, n=d_inner, tn=512, lead=(j,))
    xbc = mm(u, w_in, n=conv_dim, tn=512, col_block=d_inner // 512, lead=(j,))
    dt_raw = mm(u, w_in, n=heads, tn=LANES, col_block=(d_inner + conv_dim) // LANES, lead=(j,))
    xc, new_buf = conv_step(xbc, conv_buf, conv_w[j], conv_b[j])
    xc = _silu(xc)
    xs = xc[:, :d_inner]
    bm = xc[:, d_inner:d_inner + MB_GROUPS * n].reshape(b, MB_GROUPS, n)
    cm = xc[:, d_inner + MB_GROUPS * n:].reshape(b, MB_GROUPS, n)
    y, new_state = ssd_step(xs, dt_raw, bm, cm, state, dt_bias[j], a_log[j], d_skip[j])
    yn = prenorm(y * _silu(z), norm_g[j])
    return mm(yn, w_out, n=d, tn=256, lead=(j,)), new_state, new_buf


def gdn_sample(u, j, conv_buf, state, w_in, conv_w, dt_bias, a_log, norm_g, w_out):
    b, d = u.shape
    _, heads, dk, _ = state.shape
    width = heads * dk
    qkvg = mm(u, w_in, n=4 * width, tn=512, lead=(j,))
    ba = mm(u, _tail_cols(w_in, 4 * width, j), n=LANES, tn=LANES)
    qkv, new_buf = conv_step(qkvg[:, :3 * width], conv_buf, conv_w[j])
    qkv = _silu(qkv).reshape(b, 3, heads, dk)
    q = _l2norm(qkv[:, 0]) * (dk ** -0.5)
    k = _l2norm(qkv[:, 1])
    beta = jax.nn.sigmoid(ba[:, :heads])
    eg = jnp.exp(-jnp.exp(a_log[j]) * _softplus(ba[:, heads:2 * heads] + dt_bias[j]))
    gate = qkvg[:, 3 * width:].reshape(b, heads, dk)
    o, new_state = gdn_step(q, k, qkv[:, 2], gate, beta, eg, norm_g[j], state)
    return mm(o.reshape(b, width), w_out, n=d, tn=512, lead=(j,)), new_state, new_buf


def hgrn_sample(u, j, layer, state, w_in, lb_logits, norm_g, w_out):
    b, d = u.shape
    _, heads, dk, dv = state.shape
    qfig = mm(u, w_in, n=4 * d, tn=512, lead=(j,)).reshape(b, 4, heads, dk)
    lb_p = jax.nn.softmax(lb_logits, axis=0)
    lb = (jnp.cumsum(lb_p, axis=0) - lb_p[0])[layer].reshape(heads, dk)
    fg = lb + (1.0 - lb) * jax.nn.sigmoid(qfig[:, 1])
    q = _silu(qfig[:, 0]) * (dk ** -0.5)
    o, new_state = hgrn_step(q, 1.0 - fg, jnp.exp(jnp.log(fg)), qfig[:, 2], qfig[:, 3], norm_g[j], state)
    return mm(o.reshape(b, d), w_out, n=d, tn=512, lead=(j,)), new_state


def mla_sample(u, j, past_len, cache, page_table, w_a, q_norm, kv_norm, w_qb, w_kvb, w_o):
    b, d = u.shape
    kv_rank = kv_norm.shape[-1]
    pos = jnp.full((1,), past_len, jnp.int32)
    q_nope, q_pe, ckv, k_pe, rope_dim = mla_project(u, pos, j, w_a, q_norm, kv_norm, w_qb)
    new_rows = jnp.concatenate([ckv, k_pe[:, :rope_dim]], axis=-1)
    q_lat = jnp.swapaxes(latent_in(q_nope, w_kvb, j), 0, 1)
    q_cat = jnp.concatenate([q_lat.astype(BF16), q_pe.reshape(b, ML_HEADS, rope_dim)], axis=-1)
    o_lat = mla_decode(q_cat, new_rows[:, None, :], jnp.swapaxes(cache, 2, 3), j, page_table, kv_rank)
    o = latent_out(jnp.swapaxes(o_lat, 0, 1), w_kvb, j)
    return mm(o, w_o, n=d, tn=512, lead=(j,)), new_rows[:, None, :]


N_MIXERS = 4


_STATE_KEYS = ("ssm", "ssm_conv", "gdn", "gdn_conv", "hgrn", "mla")


def _trunk(xp, xs, bp, tp, st, p):
    norm_g = p["norm_g"]
    depth = norm_g.shape[0]
    outs_p = {k: [] for k in _STATE_KEYS}
    outs_s = {k: [] for k in _STATE_KEYS}
    ffn_w = (p["ffn_w_gate"], p["ffn_w_up"], p["ffn_w_down"])
    hp, hs = xp, xs
    xnp, xns = prenorm(hp, norm_g[0, 0]), prenorm(hs, norm_g[0, 0])
    for i in range(depth):
        mix_id, j = i % N_MIXERS, i // N_MIXERS
        g = norm_g[i]
        yp, ys = ffn(xnp, xns, *ffn_w, i, 0)
        hp, up = sandwich(hp, yp, g[1], g[2], 0.5)
        hs, us = sandwich(hs, ys, g[1], g[2], 0.5)
        if mix_id == 0:
            args = (p["mb_w_in"], p["mb_conv_w"], p["mb_conv_b"], p["mb_dt_bias"], p["mb_a_log"],
                    p["mb_d"], p["mb_norm"], p["mb_w_out"])
            mix_p, s1, c1 = mamba_prompt(up, bp, tp, j, *args)
            outs_p["ssm"].append(s1)
            outs_p["ssm_conv"].append(c1)
            mix_s, s1, c1 = mamba_sample(us, j, st["ssm_conv"][j], st["ssm"][j], *args)
            outs_s["ssm"].append(s1)
            outs_s["ssm_conv"].append(c1)
        elif mix_id == 1:
            args = (p["gd_w_in"], p["gd_conv_w"], p["gd_dt_bias"], p["gd_a_log"], p["gd_norm"],
                    p["gd_w_out"])
            mix_p, s1, c1 = gdn_prompt_mixer(up, bp, tp, j, *args)
            outs_p["gdn"].append(s1)
            outs_p["gdn_conv"].append(c1)
            mix_s, s1, c1 = gdn_sample(us, j, st["gdn_conv"][j], st["gdn"][j], *args)
            outs_s["gdn"].append(s1)
            outs_s["gdn_conv"].append(c1)
        elif mix_id == 2:
            args = (p["hg_w_in"], p["hg_lb_logits"], p["hg_norm"], p["hg_w_out"])
            mix_p, s1 = hgrn_prompt_mixer(up, bp, tp, j, i, *args)
            outs_p["hgrn"].append(s1)
            mix_s, s1 = hgrn_sample(us, j, i, st["hgrn"][j], *args)
            outs_s["hgrn"].append(s1)
        else:
            args = (p["ml_w_a"], p["ml_q_norm"], p["ml_kv_norm"], p["ml_w_qb"], p["ml_w_kvb"],
                    p["ml_w_o"])
            mix_p, rows = mla_prompt_mixer(up, bp, tp, j, *args)
            outs_p["mla"].append(rows)
            mix_s, rows = mla_sample(us, j, st["past_len"], st["cache"], st["page_table"], *args)
            outs_s["mla"].append(rows)
        hp, xnp = sandwich(hp, mix_p, g[3], g[4], 1.0)
        hs, xns = sandwich(hs, mix_s, g[3], g[4], 1.0)
        yp, ys = ffn(xnp, xns, *ffn_w, i, 1)
        g_next = norm_g[i + 1, 0] if i + 1 < depth else None
        hp, xnp = sandwich(hp, yp, g[5], g_next, 0.5)
        hs, xns = sandwich(hs, ys, g[5], g_next, 0.5)
    stack = lambda outs: tuple(jnp.stack(outs[k]) for k in _STATE_KEYS)
    return (hp, hs) + stack(outs_p) + stack(outs_s)


def kernel(x_prompt, x_sample, state_ssm, state_ssm_conv, state_gdn, state_gdn_conv, state_hgrn, cache_mla, page_table, norm_g, ffn_w_gate, ffn_w_up, ffn_w_down, mb_w_in, mb_conv_w, mb_conv_b, mb_dt_bias, mb_a_log, mb_d, mb_norm, mb_w_out, gd_w_in, gd_conv_w, gd_dt_bias, gd_a_log, gd_norm, gd_w_out, hg_w_in, hg_lb_logits, hg_norm, hg_w_out, ml_w_a, ml_q_norm, ml_kv_norm, ml_w_qb, ml_w_kvb, ml_w_o):
    p = dict(norm_g=norm_g, ffn_w_gate=ffn_w_gate, ffn_w_up=ffn_w_up, ffn_w_down=ffn_w_down,
             mb_w_in=mb_w_in, mb_conv_w=mb_conv_w, mb_conv_b=mb_conv_b, mb_dt_bias=mb_dt_bias,
             mb_a_log=mb_a_log, mb_d=mb_d, mb_norm=mb_norm, mb_w_out=mb_w_out,
             gd_w_in=gd_w_in, gd_conv_w=gd_conv_w, gd_dt_bias=gd_dt_bias, gd_a_log=gd_a_log,
             gd_norm=gd_norm, gd_w_out=gd_w_out,
             hg_w_in=hg_w_in, hg_lb_logits=hg_lb_logits, hg_norm=hg_norm, hg_w_out=hg_w_out,
             ml_w_a=ml_w_a, ml_q_norm=ml_q_norm, ml_kv_norm=ml_kv_norm, ml_w_qb=ml_w_qb,
             ml_w_kvb=ml_w_kvb, ml_w_o=ml_w_o)
    bp, tp, d = x_prompt.shape
    bs, ts, _ = x_sample.shape
    assert ts == 1
    st = dict(ssm=state_ssm, ssm_conv=state_ssm_conv, gdn=state_gdn, gdn_conv=state_gdn_conv,
              hgrn=state_hgrn, cache=cache_mla, page_table=page_table,
              past_len=page_table.shape[1] * cache_mla.shape[2])
    out = _trunk(x_prompt.reshape(bp * tp, d), x_sample.reshape(bs * ts, d), bp, tp, st, p)
    return (out[0].reshape(bp, tp, d), out[1].reshape(bs, ts, d)) + out[2:]
```

```python
import functools
import math

import jax
import jax.numpy as jnp
from jax import lax
from jax.experimental import pallas as pl
from jax.experimental.pallas import tpu as pltpu

F32 = jnp.float32
BF16 = jnp.bfloat16
HIGHEST = lax.Precision.HIGHEST

LANES = 128
VMEM_LIMIT_BYTES = 60 * 1024 * 1024

NORM_EPS = 1e-6
CONV_W = 4
CHUNK = 64
HG_CHUNK = 32
MB_HEADDIM = 64
MB_GROUPS = 8
ML_HEADS = 32
ML_NOPE = 128
ML_V = 128
ML_ROPE_THETA = 10000.0


def _cparams(sem):
    return pltpu.CompilerParams(dimension_semantics=sem, vmem_limit_bytes=VMEM_LIMIT_BYTES)


def _silu(x):
    return x * jax.nn.sigmoid(x)


def _softplus(x):
    return jnp.maximum(x, 0.0) + jnp.log1p(jnp.exp(-jnp.abs(x)))


def _bdot(a, b):
    return jnp.dot(a.astype(BF16), b.astype(BF16), preferred_element_type=F32)


def _bdot_nt(a, b):
    return lax.dot_general(a.astype(BF16), b.astype(BF16), (((1,), (1,)), ((), ())),
                           preferred_element_type=F32)


def _fdot(a, b):
    return jnp.dot(a, b, preferred_element_type=F32, precision=HIGHEST)


def _rms(x, g):
    return x * lax.rsqrt(jnp.mean(x * x, axis=-1, keepdims=True) + NORM_EPS) * g


def _prenorm_kernel(x_ref, g_ref, o_ref):
    o_ref[...] = _rms(x_ref[...].astype(F32), g_ref[...]).astype(o_ref.dtype)


def prenorm(x, g, out_dtype=BF16, col_block=0, width=None, tm=256):
    m = x.shape[0]
    d = width or x.shape[1]
    tm = min(tm, m)
    return pl.pallas_call(
        _prenorm_kernel,
        out_shape=jax.ShapeDtypeStruct((m, d), out_dtype),
        grid=(m // tm,),
        in_specs=[pl.BlockSpec((tm, d), lambda i: (i, col_block)),
                  pl.BlockSpec((1, d), lambda i: (0, 0))],
        out_specs=pl.BlockSpec((tm, d), lambda i: (i, 0)),
        compiler_params=_cparams(("parallel",)),
        name="prenorm",
    )(x, g.reshape(1, d))


def _sandwich_kernel(h_ref, y_ref, gpost_ref, gpre_ref, hn_ref, x_ref, *, scale):
    hn = h_ref[...] + scale * _rms(y_ref[...], gpost_ref[...])
    hn_ref[...] = hn
    x_ref[...] = _rms(hn, gpre_ref[...]).astype(x_ref.dtype)


def _sandwich_last_kernel(h_ref, y_ref, gpost_ref, hn_ref, *, scale):
    hn_ref[...] = h_ref[...] + scale * _rms(y_ref[...], gpost_ref[...])


def sandwich(h, y, g_post, g_pre, scale, tm=128):
    m, d = h.shape
    tm = min(tm, m)
    row = pl.BlockSpec((tm, d), lambda i: (i, 0))
    vec = pl.BlockSpec((1, d), lambda i: (0, 0))
    if g_pre is None:
        return pl.pallas_call(
            functools.partial(_sandwich_last_kernel, scale=scale),
            out_shape=jax.ShapeDtypeStruct((m, d), F32),
            grid=(m // tm,), in_specs=[row, row, vec], out_specs=row,
            compiler_params=_cparams(("parallel",)), name="sandwich_last",
        )(h, y, g_post.reshape(1, d)), None
    return pl.pallas_call(
        functools.partial(_sandwich_kernel, scale=scale),
        out_shape=(jax.ShapeDtypeStruct((m, d), F32), jax.ShapeDtypeStruct((m, d), BF16)),
        grid=(m // tm,), in_specs=[row, row, vec, vec], out_specs=(row, row),
        compiler_params=_cparams(("parallel",)), name="sandwich",
    )(h, y, g_post.reshape(1, d), g_pre.reshape(1, d))


def _ffn_rows(x_ref, wg_ref, wu_ref, wd_ref, o_ref, j, kc, nc):
    d = x_ref.shape[1]
    g = None
    u = None
    for k0 in range(0, d, kc):
        xk = x_ref[:, k0:k0 + kc]
        pg = jnp.dot(xk, wg_ref[k0:k0 + kc, :].astype(BF16), preferred_element_type=F32)
        pu = jnp.dot(xk, wu_ref[k0:k0 + kc, :].astype(BF16), preferred_element_type=F32)
        g = pg if g is None else g + pg
        u = pu if u is None else u + pu
    hmid = (_silu(g) * u).astype(BF16)

    @pl.when(j == 0)
    def _():
        o_ref[...] = jnp.zeros_like(o_ref)

    for n0 in range(0, d, nc):
        o_ref[:, n0:n0 + nc] += jnp.dot(hmid, wd_ref[:, n0:n0 + nc].astype(BF16),
                                        preferred_element_type=F32)


def _ffn_kernel(x_ref, xs_ref, wg_ref, wu_ref, wd_ref, o_ref, os_ref, *, kc, nc):
    i, j = pl.program_id(0), pl.program_id(1)
    _ffn_rows(x_ref, wg_ref, wu_ref, wd_ref, o_ref, j, kc, nc)

    @pl.when(i == 0)
    def _():
        _ffn_rows(xs_ref, wg_ref, wu_ref, wd_ref, os_ref, j, kc, nc)


def ffn(x, xs, wg, wu, wd, layer, which, tm=1024, tf=256):
    m, d = x.shape
    ms = xs.shape[0]
    f = wg.shape[-1]
    tm = min(tm, m)
    kc = min(1024, d)
    nc = min(512, d)
    once = dict(pipeline_mode=pl.Buffered(1))
    return pl.pallas_call(
        functools.partial(_ffn_kernel, kc=kc, nc=nc),
        out_shape=(jax.ShapeDtypeStruct((m, d), F32), jax.ShapeDtypeStruct((ms, d), F32)),
        grid=(m // tm, f // tf),
        in_specs=[
            pl.BlockSpec((tm, d), lambda i, j: (i, 0), **once),
            pl.BlockSpec((ms, d), lambda i, j: (0, 0), **once),
            pl.BlockSpec((None, None, d, tf), lambda i, j: (layer, which, 0, j)),
            pl.BlockSpec((None, None, d, tf), lambda i, j: (layer, which, 0, j)),
            pl.BlockSpec((None, None, tf, d), lambda i, j: (layer, which, j, 0)),
        ],
        out_specs=(pl.BlockSpec((tm, d), lambda i, j: (i, 0), **once),
                   pl.BlockSpec((ms, d), lambda i, j: (0, 0), **once)),
        compiler_params=_cparams(("arbitrary", "arbitrary")),
        name="ffn",
    )(x, xs, wg, wu, wd)


def _mm_kernel(x_ref, w_ref, o_ref, *, kc):
    k = x_ref.shape[1]
    acc = None
    for k0 in range(0, k, kc):
        p = jnp.dot(x_ref[:, k0:k0 + kc].astype(BF16), w_ref[k0:k0 + kc, :].astype(BF16),
                    preferred_element_type=F32)
        acc = p if acc is None else acc + p
    o_ref[...] = acc.astype(o_ref.dtype)


def mm(x, w, *, n, tn, col_block=0, tm=1024, out_dtype=F32, lead=None):
    m, k = x.shape
    tm = min(tm, m)
    lead = tuple(lead or ())
    assert w.ndim == 2 + len(lead) and w.shape[-2] == k and n % tn == 0 and m % tm == 0
    wspec = pl.BlockSpec((None,) * len(lead) + (k, tn), lambda i, j: lead + (0, j + col_block))
    return pl.pallas_call(
        functools.partial(_mm_kernel, kc=min(1024, k)),
        out_shape=jax.ShapeDtypeStruct((m, n), out_dtype),
        grid=(m // tm, n // tn),
        in_specs=[pl.BlockSpec((tm, k), lambda i, j: (i, 0)), wspec],
        out_specs=pl.BlockSpec((tm, tn), lambda i, j: (i, j)),
        compiler_params=_cparams(("parallel", "arbitrary")),
        name="mm",
    )(x, w)


def _iota(shape, dim):
    return lax.broadcasted_iota(jnp.int32, shape, dim)


def _conv_silu(x, w_ref, b_ref=None):
    row = _iota(x.shape, 0)
    acc = x * w_ref[CONV_W - 1:CONV_W, :]
    for k in range(1, CONV_W):
        shifted = jnp.where(row >= k, pltpu.roll(x, k, 0), 0.0)
        acc = acc + shifted * w_ref[CONV_W - 1 - k:CONV_W - k, :]
    if b_ref is not None:
        acc = acc + b_ref[...]
    return _silu(acc)


def _chunk_cumsum(x, chunk):
    pos = _iota(x.shape, 0) & (chunk - 1)
    k = 1
    while k < chunk:
        x = x + jnp.where(pos >= k, pltpu.roll(x, k, 0), 0.0)
        k *= 2
    return x


def _lane_select_matrix(sel):
    return (_iota((LANES, LANES), 0) == sel).astype(BF16)


def _split2(a):
    hi = a.astype(BF16)
    return hi, (a - hi.astype(F32)).astype(BF16)


def _expand(x, pick):
    h1 = x.astype(BF16)
    r1 = x - h1.astype(F32)
    h2 = r1.astype(BF16)
    h3 = (r1 - h2.astype(F32)).astype(BF16)
    dot = functools.partial(jnp.dot, preferred_element_type=F32)
    return dot(h1, pick) + dot(h2, pick) + dot(h3, pick)


def _dot3(a_parts, b_parts):
    (ah, al), (bh, bl) = a_parts, b_parts
    dot = functools.partial(jnp.dot, preferred_element_type=F32)
    return dot(ah, bh) + (dot(al, bh) + dot(ah, bl))


def _ssd_kernel(xs_ref, bm_ref, cm_ref, wx_ref, wb_ref, wc_ref, bx_ref, bb_ref, bc_ref,
                dt_ref, dtb_ref, alog_ref, dsk_ref, z_ref, y_ref, st_ref,
                xs_s, bm_s, cm_s, cum_s, dte_s, dt_s, la_s, *, pairs_per_group):
    hp = pl.program_id(1)
    t_len = xs_ref.shape[1]
    half = LANES // 2
    xs_s[...] = _conv_silu(xs_ref[0], wx_ref, bx_ref)

    @pl.when(hp % pairs_per_group == 0)
    def _():
        bm_s[...] = _conv_silu(bm_ref[0], wb_ref, bb_ref)
        cm_s[...] = _conv_silu(cm_ref[0], wc_ref, bc_ref)

    @pl.when(hp == 0)
    def _():
        dt = _softplus(dt_ref[0] + dtb_ref[...])
        dt_s[...] = dt
        la_s[...] = dt * (-jnp.exp(alog_ref[...]))

    lane_row = _iota((1, LANES), 1)
    pick = _lane_select_matrix(2 * hp + (lane_row >= half).astype(jnp.int32))
    dte_s[...] = _expand(dt_s[...], pick)
    cum_s[...] = _chunk_cumsum(_expand(la_s[...], pick), CHUNK)

    t_io = _iota((CHUNK, LANES), 0)
    s_io = _iota((CHUNK, LANES), 1) & (half - 1)
    causal = s_io <= t_io
    same_head = (_iota((LANES, LANES), 0) >= half) == (_iota((LANES, LANES), 1) >= half)
    low = lane_row < half
    zeros = jnp.zeros((CHUNK, LANES), F32)
    dskip = dsk_ref[...]

    blk = 2 * CHUNK
    n_blocks = t_len // blk
    group = 4 if n_blocks % 4 == 0 else 1

    def each(f, *lists):
        return [f(*args) for args in zip(*lists)]

    def row_form(m_t, cc):
        a0, a1 = m_t[0:1, :], m_t[half:half + 1, :]
        if cc == 0:
            return jnp.where(low, a0, pltpu.roll(a1, half, 1))
        return jnp.where(low, pltpu.roll(a0, half, 1), a1)

    def blocks(i, st):
        rows = [pl.ds(pl.multiple_of((group * i + b_i) * blk, blk), blk) for b_i in range(group)]
        x2, b2, c2, cu2, dt2 = ([ref[r, :] for r in rows] for ref in (xs_s, bm_s, cm_s, cum_s, dte_s))
        cu_t, dt_t, b_t = (each(lambda a: a.T, l) for l in (cu2, dt2, b2))
        ccs = [cc for _ in range(group) for cc in range(2)]
        sl = [slice(cc * CHUNK, (cc + 1) * CHUNK) for cc in ccs]
        per_chunk = lambda l: [l[n // 2][sl[n]] for n in range(2 * group)]
        xc, bc, cmc, ce, de = (per_chunk(l) for l in (x2, b2, c2, cu2, dt2))
        cum_row = [row_form(cu_t[n // 2], ccs[n]) for n in range(2 * group)]
        dt_row = [row_form(dt_t[n // 2], ccs[n]) for n in range(2 * group)]
        cb = each(lambda c, b: _bdot_nt(c, jnp.concatenate([b, b], axis=0)), cmc, bc)
        wp = each(lambda cb_, ce_, cr, dr: cb_ * jnp.exp(jnp.where(causal, ce_ - cr, -jnp.inf)) * dr,
                  cb, ce, cum_row, dt_row)
        bd = each(lambda x: jnp.where(same_head, jnp.concatenate([x, x], axis=0), 0.0), xc)
        y_intra = each(_bdot, wp, bd)
        cl = each(lambda ce_: ce_[CHUNK - 1:CHUNK, :], ce)
        tx = each(lambda cl_, ce_, de_, x: jnp.exp(cl_ - ce_) * de_ * x, cl, ce, de, xc)
        tx_pad = [jnp.concatenate([t_, zeros] if cc == 0 else [zeros, t_], axis=0)
                  for t_, cc in zip(tx, ccs)]
        upd = [_bdot(b_t[n // 2], tx_pad[n]) for n in range(2 * group)]
        dec = each(jnp.exp, cl)
        starts = []
        for n in range(2 * group):
            starts.append(st)
            st = st * dec[n] + upd[n]
        y_inter = each(_bdot, cmc, starts)
        y = each(lambda yi, ce_, yn, x: yi + jnp.exp(ce_) * yn + x * dskip, y_intra, ce, y_inter, xc)
        for b_i, r in enumerate(rows):
            y_ref[0, r, :] = jnp.concatenate(y[2 * b_i:2 * b_i + 2], axis=0) * _silu(z_ref[0, r, :])
        return st

    st = lax.fori_loop(0, n_blocks // group, blocks, jnp.zeros((LANES, LANES), F32))
    st_ref[0] = st.T


def ssd_prompt(xbc, dt_raw, z, conv_w, conv_b, dt_bias, a_log, d_skip):
    b, t, conv_dim = xbc.shape
    d_inner = z.shape[-1]
    heads = d_inner // MB_HEADDIM
    n = (conv_dim - d_inner) // (2 * MB_GROUPS)
    assert heads == LANES and n == LANES and t % (2 * CHUNK) == 0
    pairs = heads // 2
    ppg = pairs // MB_GROUPS
    xoff, boff = 0, d_inner // LANES
    coff = boff + MB_GROUPS

    def seq(off, grouped):
        if grouped:
            return lambda bi, hp: (bi, 0, off + hp // ppg)
        return lambda bi, hp: (bi, 0, off + hp)

    def par(off, grouped):
        if grouped:
            return lambda bi, hp: (0, off + hp // ppg)
        return lambda bi, hp: (0, off + hp)

    slab = lambda f: pl.BlockSpec((1, t, LANES), f)
    wsp = lambda f: pl.BlockSpec((CONV_W, LANES), f)
    vsp = lambda f: pl.BlockSpec((1, LANES), f)
    cb2 = conv_b.reshape(1, conv_dim)
    dsk = jnp.repeat(d_skip, MB_HEADDIM).reshape(1, d_inner)
    return pl.pallas_call(
        functools.partial(_ssd_kernel, pairs_per_group=ppg),
        out_shape=(jax.ShapeDtypeStruct((b, t, d_inner), F32),
                   jax.ShapeDtypeStruct((b, d_inner, n), F32)),
        grid=(b, pairs),
        in_specs=[slab(seq(xoff, False)), slab(seq(boff, True)), slab(seq(coff, True)),
                  wsp(par(xoff, False)), wsp(par(boff, True)), wsp(par(coff, True)),
                  vsp(par(xoff, False)), vsp(par(boff, True)), vsp(par(coff, True)),
                  slab(lambda bi, hp: (bi, 0, 0)), vsp(lambda bi, hp: (0, 0)), vsp(lambda bi, hp: (0, 0)),
                  vsp(lambda bi, hp: (0, hp)), slab(lambda bi, hp: (bi, 0, hp))],
        out_specs=(slab(lambda bi, hp: (bi, 0, hp)),
                   pl.BlockSpec((1, LANES, LANES), lambda bi, hp: (bi, hp, 0))),
        scratch_shapes=[pltpu.VMEM((t, LANES), F32)] * 7,
        compiler_params=_cparams(("arbitrary", "arbitrary")),
        name="ssd_prompt",
    )(xbc, xbc, xbc, conv_w, conv_w, conv_w, cb2, cb2, cb2,
      dt_raw, dt_bias.reshape(1, heads), a_log.reshape(1, heads), dsk, z)


def mamba_prompt(u, b, t, j, w_in, conv_w, conv_b, dt_bias, a_log, d_skip, norm_g, w_out):
    m, d = u.shape
    d_inner = norm_g.shape[-1]
    heads = dt_bias.shape[-1]
    conv_dim = conv_w.shape[-1]
    z = mm(u, w_in, n=d_inner, tn=512, lead=(j,))
    xbc = mm(u, w_in, n=conv_dim, tn=512, col_block=d_inner // 512, lead=(j,))
    dt_raw = mm(u, w_in, n=heads, tn=LANES, col_block=(d_inner + conv_dim) // LANES, lead=(j,))
    xbc3 = xbc.reshape(b, t, conv_dim)
    gated, state = ssd_prompt(xbc3, dt_raw.reshape(b, t, heads), z.reshape(b, t, d_inner),
                              conv_w[j], conv_b[j], dt_bias[j], a_log[j], d_skip[j])
    yn = prenorm(gated.reshape(m, d_inner), norm_g[j], tm=128)
    mix = mm(yn, w_out, n=d, tn=256, tm=1024, lead=(j,))
    n = state.shape[-1]
    return mix, state.reshape(b, heads, MB_HEADDIM, n), xbc3[:, t - (CONV_W - 1):, :]


def _l2norm(x):
    return x * lax.rsqrt(jnp.sum(x * x, axis=-1, keepdims=True) + NORM_EPS)


def _gdn_kernel(q_ref, k_ref, v_ref, gate_ref, wq_ref, wk_ref, wv_ref, ba_ref, dtb_ref, alog_ref,
                g_ref, o_ref, st_ref, q_s, k_s, v_s, beta_s, cum_s, u_s, w_s, a_s, *, heads, hps):
    hg = pl.program_id(1)
    t_len = q_ref.shape[1]
    dk = q_ref.shape[2] // hps
    ba = ba_ref[0]
    beta_all = jax.nn.sigmoid(ba)
    lg_all = -jnp.exp(alog_ref[...]) * _softplus(ba + dtb_ref[...])
    for hh in range(hps):
        cols = slice(hh * dk, (hh + 1) * dk)
        h = hg * hps + hh
        q_s[hh] = _l2norm(_conv_silu(q_ref[0, :, cols], wq_ref[:, cols])) * (dk ** -0.5)
        k_s[hh] = _l2norm(_conv_silu(k_ref[0, :, cols], wk_ref[:, cols]))
        v_s[hh] = _conv_silu(v_ref[0, :, cols], wv_ref[:, cols])
        beta_s[hh] = _expand(beta_all, _lane_select_matrix(h))
        cum_s[hh] = _chunk_cumsum(_expand(lg_all, _lane_select_matrix(heads + h)), CHUNK)

    blk = 2 * CHUNK
    r_io, c_io = _iota((blk, blk), 0), _iota((blk, blk), 1)
    same_chunk = (r_io >= CHUNK) == (c_io >= CHUNK)
    strict = same_chunk & (c_io < r_io)
    incl = same_chunk & (c_io <= r_io)
    eye = (r_io == c_io).astype(F32)
    lane_row = _iota((1, blk), 1)
    zeros = jnp.zeros((CHUNK, dk), F32)

    n_blocks = t_len // blk
    group = 4 if n_blocks % 4 == 0 else 1

    def each(f, *lists):
        return [f(*args) for args in zip(*lists)]

    def solve_blocks(hh, i, carry):
        rows = [pl.ds(pl.multiple_of((group * i + b_i) * blk, blk), blk) for b_i in range(group)]
        q2, k2, v2 = ([ref[hh, r, :] for r in rows] for ref in (q_s, k_s, v_s))
        be2, g2 = ([ref[hh, r, :] for r in rows] for ref in (beta_s, cum_s))
        diff = each(lambda g: g - g.T[0:1, :], g2)
        n1 = each(lambda be, k, d: -(be * _bdot_nt(k, k) * jnp.exp(jnp.where(strict, d, -jnp.inf))),
                  be2, k2, diff)
        square = lambda s: _dot3(s, s)
        pair = lambda a, b, sa, sb: eye + a + b + _dot3(sa, sb)
        s1 = each(_split2, n1)
        n2 = each(square, s1)
        s2 = each(_split2, n2)
        n4 = each(square, s2)
        f12 = each(pair, n1, n2, s1, s2)
        s4 = each(_split2, n4)
        n8 = each(square, s4)
        s8 = each(_split2, n8)
        n16 = each(square, s8)
        f48 = each(pair, n4, n8, s4, s8)
        s16 = each(_split2, n16)
        n32 = each(square, s16)
        f1632 = each(pair, n16, n32, s16, each(_split2, n32))
        f1248 = each(_dot3, each(_split2, f12), each(_split2, f48))
        t_inv = each(_dot3, each(_split2, f1248), each(_split2, f1632))
        rhs = each(lambda v, k, be, g: jnp.concatenate([v * be, k * (be * jnp.exp(g))], axis=1),
                   v2, k2, be2, g2)
        uw = each(_dot3, each(_split2, t_inv), each(_split2, rhs))
        attn = each(lambda q, k, d: _bdot_nt(q, k) * jnp.exp(jnp.where(incl, d, -jnp.inf)),
                    q2, k2, diff)
        for r, uw_b, a_b in zip(rows, uw, attn):
            u_s[hh, r, :] = uw_b[:, :dk]
            w_s[hh, r, :] = uw_b[:, dk:]
            a_s[hh, r, :] = a_b
        return carry

    for hh in range(hps):
        lax.fori_loop(0, n_blocks // group, functools.partial(solve_blocks, hh), 0)

    def pair_of_chunks(i, sts):
        rows = pl.ds(pl.multiple_of(i * blk, blk), blk)
        hs = list(range(hps))
        sts = list(sts)
        q2, k2, g2 = ([ref[hh, rows, :] for hh in hs] for ref in (q_s, k_s, cum_s))
        u2, w2 = ([ref[hh, rows, :] for hh in hs] for ref in (u_s, w_s))
        g_row = each(lambda g: g.T[0:1, :], g2)
        k_t = each(lambda k: k.T, k2)
        qe2 = each(lambda q, g: q * jnp.exp(g), q2, g2)
        vns = [[] for _ in hs]
        o_inter = [[] for _ in hs]
        for cc in range(2):
            sl = slice(cc * CHUNK, (cc + 1) * CHUNK)
            in_chunk = (lane_row >= cc * CHUNK) & (lane_row < (cc + 1) * CHUNK)
            vn = [u[sl] - _bdot(w[sl], st) for u, w, st in zip(u2, w2, sts)]
            for hh in hs:
                o_inter[hh].append(_bdot(qe2[hh][sl], sts[hh]))
            g_last = [g[(cc + 1) * CHUNK - 1:(cc + 1) * CHUNK, :] for g in g2]
            kd = [kt * jnp.where(in_chunk, jnp.exp(gl - gr), 0.0)
                  for kt, gl, gr in zip(k_t, g_last, g_row)]
            vn_pad = [jnp.concatenate([v, zeros] if cc == 0 else [zeros, v], axis=0) for v in vn]
            sts = [st * jnp.exp(gl) + _bdot(kd_, vp) for st, gl, kd_, vp in zip(sts, g_last, kd, vn_pad)]
            for hh in hs:
                vns[hh].append(vn[hh])
        for hh in hs:
            cols = slice(hh * dk, (hh + 1) * dk)
            o2 = (jnp.concatenate(o_inter[hh], axis=0)
                  + _bdot(a_s[hh, rows, :], jnp.concatenate(vns[hh], axis=0)))
            o_ref[0, rows, cols] = (_rms(o2, g_ref[...])
                                    * _silu(gate_ref[0, rows, cols])).astype(o_ref.dtype)
        return tuple(sts)

    sts = lax.fori_loop(0, t_len // blk, pair_of_chunks,
                        tuple(jnp.zeros((dk, dk), F32) for _ in range(hps)))
    for hh in range(hps):
        st_ref[0, hh] = sts[hh]


GDN_HEADS_PER_STEP = 2


def gdn_prompt(qkvg, ba, conv_w, dt_bias, a_log, norm_g):
    b, t, w4 = qkvg.shape
    width = w4 // 4
    dk = norm_g.shape[-1]
    heads = width // dk
    hps = GDN_HEADS_PER_STEP
    assert dk == LANES and 2 * heads <= LANES and t % (2 * CHUNK) == 0 and heads % hps == 0
    pad = LANES - 2 * heads
    dtb = jnp.pad(dt_bias, (heads, pad)).reshape(1, LANES)
    alog = jnp.pad(a_log, (heads, pad)).reshape(1, LANES)
    hgs = heads // hps
    slab = lambda seg: pl.BlockSpec((1, t, hps * dk), lambda bi, h: (bi, 0, seg * hgs + h))
    wsp = lambda seg: pl.BlockSpec((CONV_W, hps * dk), lambda bi, h: (0, seg * hgs + h))
    vec = pl.BlockSpec((1, LANES), lambda bi, h: (0, 0))
    return pl.pallas_call(
        functools.partial(_gdn_kernel, heads=heads, hps=hps),
        out_shape=(jax.ShapeDtypeStruct((b, t, width), BF16),
                   jax.ShapeDtypeStruct((b, heads, dk, dk), F32)),
        grid=(b, hgs),
        in_specs=[slab(0), slab(1), slab(2), slab(3), wsp(0), wsp(1), wsp(2),
                  pl.BlockSpec((1, t, LANES), lambda bi, h: (bi, 0, 0)), vec, vec, vec],
        out_specs=(pl.BlockSpec((1, t, hps * dk), lambda bi, h: (bi, 0, h)),
                   pl.BlockSpec((1, hps, dk, dk), lambda bi, h: (bi, h, 0, 0))),
        scratch_shapes=[pltpu.VMEM((hps, t, dk), F32)] * 8,
        compiler_params=_cparams(("parallel", "arbitrary")),
        name="gdn_prompt",
    )(qkvg, qkvg, qkvg, qkvg, conv_w, conv_w, conv_w, ba, dtb, alog, norm_g.reshape(1, dk))


def _tail_cols(w, start, lead):
    cols = w[lead][:, start:]
    return jnp.pad(cols, ((0, 0), (0, LANES - cols.shape[1])))


def gdn_prompt_mixer(u, b, t, j, w_in, conv_w, dt_bias, a_log, norm_g, w_out):
    m, d = u.shape
    width = conv_w.shape[-1] // 3
    qkvg = mm(u, w_in, n=4 * width, tn=512, lead=(j,))
    ba = mm(u, _tail_cols(w_in, 4 * width, j), n=LANES, tn=LANES)
    o, state = gdn_prompt(qkvg.reshape(b, t, 4 * width), ba.reshape(b, t, LANES), conv_w[j],
                          dt_bias[j], a_log[j], norm_g[j])
    mix = mm(o.reshape(m, width), w_out, n=d, tn=512, lead=(j,))
    new_buf = qkvg.reshape(b, t, 4 * width)[:, t - (CONV_W - 1):, :3 * width]
    return mix, state, new_buf


def _hgrn_lower_bound(lb_ref, layer):
    logits = lb_ref[...]
    e = jnp.exp(logits - jnp.max(logits, axis=0, keepdims=True))
    p = e / jnp.sum(e, axis=0, keepdims=True)
    lb = jnp.zeros((1, logits.shape[1]), F32)
    for r in range(1, layer + 1):
        lb = lb + p[r:r + 1, :]
    return lb


def _hgrn_kernel(q_ref, f_ref, i_ref, g_ref, lb_ref, ng_ref, o_ref, st_ref,
                 q_s, k_s, cum_s, *, layer):
    t_len = q_ref.shape[1]
    dk = q_ref.shape[2]
    lb = _hgrn_lower_bound(lb_ref, layer)
    fg = lb + (1.0 - lb) * jax.nn.sigmoid(f_ref[0])
    q_s[...] = _silu(q_ref[0]) * (dk ** -0.5)
    k_s[...] = 1.0 - fg
    cum_s[...] = _chunk_cumsum(jnp.log(fg), HG_CHUNK)

    blk = 4 * HG_CHUNK
    t_io = _iota((HG_CHUNK, dk), 0)
    lane = _iota((HG_CHUNK, blk), 1)
    row_blk = _iota((blk, dk), 0)

    def block_of_chunks(i, st_t):
        r0 = pl.multiple_of(i * blk, blk)
        rows = pl.ds(r0, blk)
        v4 = i_ref[0, rows, :]
        v4_t = v4.T
        outs = []
        for cc in range(4):
            c0 = cc * HG_CHUNK
            crow = pl.ds(r0 + c0, HG_CHUNK)
            qc, kc, cu = q_s[crow, :], k_s[crow, :], cum_s[crow, :]
            attn = jnp.zeros((HG_CHUNK, blk), F32)
            for s in range(HG_CHUNK):
                srow = pl.ds(r0 + c0 + s, 1)
                decay = jnp.exp(jnp.where(t_io >= s, cu - cum_s[srow, :], -jnp.inf))
                col = jnp.sum(qc * k_s[srow, :] * decay, axis=1, keepdims=True)
                attn = jnp.where(lane == c0 + s, col, attn)
            o = _bdot(attn, v4) + _bdot_nt(qc * jnp.exp(cu), st_t)
            outs.append(o)
            last = cu[HG_CHUNK - 1:HG_CHUNK, :]
            in_chunk = (row_blk >= c0) & (row_blk < c0 + HG_CHUNK)
            kd = jnp.where(in_chunk, k_s[rows, :] * jnp.exp(last - cum_s[rows, :]), 0.0)
            st_t = st_t * jnp.exp(last) + _bdot(v4_t, kd)
        o4 = jnp.concatenate(outs, axis=0)
        o_ref[0, rows, :] = (_rms(o4, ng_ref[...]) * _silu(g_ref[0, rows, :])).astype(o_ref.dtype)
        return st_t

    st_t = lax.fori_loop(0, t_len // blk, block_of_chunks, jnp.zeros((dk, dk), F32))
    st_ref[0, 0] = st_t.T


def hgrn_prompt(qfig, lb_logits, norm_g, layer):
    b, t, d4 = qfig.shape
    d = d4 // 4
    dk = norm_g.shape[-1]
    heads = d // dk
    depth = lb_logits.shape[0]
    assert dk == LANES and t % (4 * HG_CHUNK) == 0
    slab = lambda off: pl.BlockSpec((1, t, dk), lambda bi, h: (bi, 0, off + h))
    return pl.pallas_call(
        functools.partial(_hgrn_kernel, layer=layer),
        out_shape=(jax.ShapeDtypeStruct((b, t, d), BF16),
                   jax.ShapeDtypeStruct((b, heads, dk, dk), F32)),
        grid=(b, heads),
        in_specs=[slab(0), slab(heads), slab(2 * heads), slab(3 * heads),
                  pl.BlockSpec((depth, dk), lambda bi, h: (0, h)),
                  pl.BlockSpec((1, dk), lambda bi, h: (0, 0))],
        out_specs=(pl.BlockSpec((1, t, dk), lambda bi, h: (bi, 0, h)),
                   pl.BlockSpec((1, 1, dk, dk), lambda bi, h: (bi, h, 0, 0))),
        scratch_shapes=[pltpu.VMEM((t, dk), F32)] * 3,
        compiler_params=_cparams(("parallel", "arbitrary")),
        name="hgrn_prompt",
    )(qfig, qfig, qfig, qfig, lb_logits, norm_g.reshape(1, dk))


def hgrn_prompt_mixer(u, b, t, j, layer, w_in, lb_logits, norm_g, w_out):
    m, d = u.shape
    qfig = mm(u, w_in, n=4 * d, tn=512, lead=(j,))
    o, state = hgrn_prompt(qfig.reshape(b, t, 4 * d), lb_logits, norm_g[j], layer)
    mix = mm(o.reshape(m, d), w_out, n=d, tn=512, lead=(j,))
    return mix, state


def rope_tables(pos, rope_dim):
    half = rope_dim // 2
    inv = ML_ROPE_THETA ** (-jnp.arange(half, dtype=F32) / half)
    ang = pos.astype(F32)[:, None] * inv
    cos, sin = jnp.cos(ang), jnp.sin(ang)
    reps = LANES // rope_dim
    return (jnp.tile(jnp.concatenate([cos, cos], axis=-1), (1, reps)),
            jnp.tile(jnp.concatenate([-sin, sin], axis=-1), (1, reps)))


def _rope_kernel(x_ref, cos_ref, sin_ref, o_ref, *, rope_dim, dup):
    half = rope_dim // 2
    lane = _iota((x_ref.shape[0], LANES), 1)
    first = (lane & (rope_dim - 1)) < half
    cos, sin = cos_ref[...], sin_ref[...]
    for c in range(x_ref.shape[1] // LANES):
        x = x_ref[:, c * LANES:(c + 1) * LANES]
        other = jnp.where(first, pltpu.roll(x, LANES - half, 1), pltpu.roll(x, half, 1))
        y = x * cos + other * sin
        if dup:
            y = y + pltpu.roll(y, rope_dim, 1)
        o_ref[:, c * LANES:(c + 1) * LANES] = y.astype(o_ref.dtype)


def rope(x, cos, sin, rope_dim, out_dtype, dup=False, tm=256):
    m, w = x.shape
    t = cos.shape[0]
    if t == 1:
        tm = min(tm, m)
        tab = pl.BlockSpec((1, LANES), lambda i: (0, 0))
    else:
        tm = min(tm, t)
        nt = t // tm
        tab = pl.BlockSpec((tm, LANES), lambda i: (i % nt, 0))
    return pl.pallas_call(
        functools.partial(_rope_kernel, rope_dim=rope_dim, dup=dup),
        out_shape=jax.ShapeDtypeStruct((m, w), out_dtype),
        grid=(m // tm,),
        in_specs=[pl.BlockSpec((tm, w), lambda i: (i, 0)), tab, tab],
        out_specs=pl.BlockSpec((tm, w), lambda i: (i, 0)),
        compiler_params=_cparams(("parallel",)),
        name="rope",
    )(x, cos, sin)


def _mla_attn_kernel(qn_ref, qp_ref, kn_ref, kp_ref, v_ref, o_ref, *, tq, scale, rope_dim):
    h = pl.program_id(1)
    t_len = qn_ref.shape[1]
    lane = _iota((tq, LANES), 1)
    mine = (lane // rope_dim) == (h % (LANES // rope_dim))
    for q0 in range(0, t_len, tq):
        kv_len = q0 + tq
        qn = qn_ref[0, q0:q0 + tq, :]
        qp = jnp.where(mine, qp_ref[0, q0:q0 + tq, :], 0.0)
        s = (_bdot_nt(qn, kn_ref[0, :kv_len, :]) + _bdot_nt(qp, kp_ref[0, :kv_len, :])) * scale
        ok = _iota((tq, kv_len), 1) <= _iota((tq, kv_len), 0) + q0
        s = jnp.where(ok, s, -jnp.inf)
        p = jnp.exp(s - jnp.max(s, axis=-1, keepdims=True))
        l = jnp.sum(p, axis=-1, keepdims=True)
        o_ref[0, q0:q0 + tq, :] = (_bdot(p, v_ref[0, :kv_len, :]) / l).astype(o_ref.dtype)


def mla_prompt_attention(qn, qp, kv, kp, rope_dim):
    b, t, w = qn.shape
    heads = w // ML_NOPE
    tq = min(512, t)
    scale = (ML_NOPE + rope_dim) ** -0.5
    per_tile = LANES // rope_dim
    slab = lambda f: pl.BlockSpec((1, t, LANES), f)
    return pl.pallas_call(
        functools.partial(_mla_attn_kernel, tq=tq, scale=scale, rope_dim=rope_dim),
        out_shape=jax.ShapeDtypeStruct((b, t, heads * ML_V), BF16),
        grid=(b, heads),
        in_specs=[slab(lambda bi, h: (bi, 0, h)), slab(lambda bi, h: (bi, 0, h // per_tile)),
                  slab(lambda bi, h: (bi, 0, 2 * h)), slab(lambda bi, h: (bi, 0, 0)),
                  slab(lambda bi, h: (bi, 0, 2 * h + 1))],
        out_specs=slab(lambda bi, h: (bi, 0, h)),
        compiler_params=_cparams(("parallel", "arbitrary")),
        name="mla_attn",
    )(qn, qp, kv, kp, kv)


def mla_project(u, pos, j, w_a, q_norm, kv_norm, w_qb):
    q_rank, kv_rank = q_norm.shape[-1], kv_norm.shape[-1]
    rope_dim = w_a.shape[-1] - q_rank - kv_rank
    a1 = mm(u, w_a, n=q_rank + kv_rank, tn=512, lead=(j,))
    kpe_raw = mm(u, _tail_cols(w_a, q_rank + kv_rank, j), n=LANES, tn=LANES)
    q_c = prenorm(a1, q_norm[j], width=q_rank)
    ckv = prenorm(a1, kv_norm[j], out_dtype=F32, col_block=q_rank // kv_rank, width=kv_rank)
    cos, sin = rope_tables(pos, rope_dim)
    k_pe = rope(kpe_raw, cos, sin, rope_dim, F32, dup=True)
    wq = w_qb[j].reshape(q_rank, ML_HEADS, ML_NOPE + rope_dim)
    w_nope = wq[:, :, :ML_NOPE].reshape(q_rank, ML_HEADS * ML_NOPE)
    w_pe = wq[:, :, ML_NOPE:].reshape(q_rank, ML_HEADS * rope_dim)
    q_nope = mm(q_c, w_nope, n=ML_HEADS * ML_NOPE, tn=512, out_dtype=BF16)
    q_pe = rope(mm(q_c, w_pe, n=ML_HEADS * rope_dim, tn=512), cos, sin, rope_dim, BF16)
    return q_nope, q_pe, ckv, k_pe, rope_dim


def mla_prompt_mixer(u, b, t, j, w_a, q_norm, kv_norm, w_qb, w_kvb, w_o):
    m, d = u.shape
    q_nope, q_pe, ckv, k_pe, rope_dim = mla_project(u, jnp.arange(t), j, w_a, q_norm, kv_norm, w_qb)
    kv = mm(ckv, w_kvb, n=w_kvb.shape[-1], tn=512, out_dtype=BF16, lead=(j,))
    o = mla_prompt_attention(q_nope.reshape(b, t, -1), q_pe.reshape(b, t, -1), kv.reshape(b, t, -1),
                             k_pe.astype(BF16).reshape(b, t, LANES), rope_dim)
    mix = mm(o.reshape(m, -1), w_o, n=d, tn=512, lead=(j,))
    rows = jnp.concatenate([ckv, k_pe[:, :rope_dim]], axis=-1).reshape(b, t, -1)
    return mix, rows


def _lane_pick(x, onehot):
    return jnp.sum(jnp.where(onehot, x, 0.0), axis=1, keepdims=True)


_HEADS_PER_ITER_SSD = 8
_HEADS_PER_ITER = 4


def conv_step(x, buf, w, b=None):
    xp = jnp.concatenate([buf, x[:, None, :]], axis=1)
    y = xp[:, 0] * w[0]
    for k in range(1, CONV_W):
        y = y + xp[:, k] * w[k]
    if b is not None:
        y = y + b
    return y, xp[:, 1:]


def _ssd_step_kernel(xs_ref, dt_ref, dtb_ref, alog_ref, dsk_ref, bm_ref, cm_ref, s_ref,
                     y_ref, so_ref, *, heads_per_group):
    heads = s_ref.shape[1]
    xs_t = xs_ref[0]
    dt = _softplus(dt_ref[0] + dtb_ref[...])
    dec = jnp.exp(dt * (-jnp.exp(alog_ref[...])))
    dtxs = xs_t * dt
    lane = _iota((1, heads), 1)
    cb = jnp.zeros((1, heads), F32)
    for g in range(bm_ref.shape[1]):
        cb_g = jnp.sum(cm_ref[0, g:g + 1, :] * bm_ref[0, g:g + 1, :], axis=1, keepdims=True)
        cb = jnp.where(lane // heads_per_group == g, cb_g, cb)

    hb = _HEADS_PER_ITER_SSD
    assert heads_per_group % hb == 0

    def head_block(i, y_acc):
        hs = [i * hb + k for k in range(hb)]
        g = (i * hb) // heads_per_group
        cm_row, bm_row = cm_ref[0, pl.ds(g, 1), :], bm_ref[0, pl.ds(g, 1), :]
        s = [s_ref[0, h] for h in hs]
        onehot = [lane == h for h in hs]
        dec_h = [_lane_pick(dec, o) for o in onehot]
        dx_col = [_lane_pick(dtxs, o) for o in onehot]
        y_col = [jnp.sum(s_ * cm_row, axis=1, keepdims=True) * d for s_, d in zip(s, dec_h)]
        for h, s_, d, x in zip(hs, s, dec_h, dx_col):
            so_ref[0, h] = s_ * d + x * bm_row
        for o, y in zip(onehot, y_col):
            y_acc = jnp.where(o, y, y_acc)
        return y_acc

    y_acc = lax.fori_loop(0, heads // hb, head_block, jnp.zeros(xs_t.shape, F32))
    y_ref[0] = y_acc + cb * dtxs + xs_t * dsk_ref[...]


def ssd_step(xs, dt_raw, bm, cm, state, dt_bias, a_log, d_skip):
    b, heads, p, n = state.shape
    groups = bm.shape[1]
    xs_t = jnp.swapaxes(xs.reshape(b, heads, p), 1, 2)
    row = lambda a: a.reshape(1, heads)
    vec = pl.BlockSpec((1, heads), lambda i: (0, 0))
    y_t, new_state = pl.pallas_call(
        functools.partial(_ssd_step_kernel, heads_per_group=heads // groups),
        out_shape=(jax.ShapeDtypeStruct((b, p, heads), F32), jax.ShapeDtypeStruct(state.shape, F32)),
        grid=(b,),
        in_specs=[pl.BlockSpec((1, p, heads), lambda i: (i, 0, 0)),
                  pl.BlockSpec((1, 1, heads), lambda i: (i, 0, 0)), vec, vec, vec,
                  pl.BlockSpec((1, groups, n), lambda i: (i, 0, 0)),
                  pl.BlockSpec((1, groups, n), lambda i: (i, 0, 0)),
                  pl.BlockSpec((1, heads, p, n), lambda i: (i, 0, 0, 0))],
        out_specs=(pl.BlockSpec((1, p, heads), lambda i: (i, 0, 0)),
                   pl.BlockSpec((1, heads, p, n), lambda i: (i, 0, 0, 0))),
        compiler_params=_cparams(("parallel",)),
        name="ssd_step",
    )(xs_t, dt_raw.reshape(b, 1, heads), row(dt_bias), row(a_log), row(d_skip), bm, cm, state)
    return jnp.swapaxes(y_t, 1, 2).reshape(b, heads * p), new_state


def _gdn_step_kernel(q_ref, k_ref, v_ref, gate_ref, beta_ref, eg_ref, ng_ref, s_ref, o_ref, so_ref):
    heads = s_ref.shape[1]
    q_t, k_t = q_ref[0], k_ref[0]
    lane = _iota((1, heads), 1)

    hb = _HEADS_PER_ITER
    col_sum = lambda x: jnp.sum(x, axis=0, keepdims=True)

    def head_block(i, carry):
        hs = [i * hb + k for k in range(hb)]
        s = [s_ref[0, h] for h in hs]
        onehot = [lane == h for h in hs]
        q_col = [_lane_pick(q_t, o) for o in onehot]
        k_col = [_lane_pick(k_t, o) for o in onehot]
        beta = [_lane_pick(beta_ref[0], o) for o in onehot]
        eg = [_lane_pick(eg_ref[0], o) for o in onehot]
        v_row = [v_ref[0, pl.ds(h, 1), :] for h in hs]
        vn = [v * b - col_sum((k * (b * e)) * s_) for v, b, e, k, s_ in zip(v_row, beta, eg, k_col, s)]
        qk = [col_sum(q * k) for q, k in zip(q_col, k_col)]
        o = [e * col_sum(q * s_) + a * n for e, q, s_, a, n in zip(eg, q_col, s, qk, vn)]
        for h, s_, e, k, n, o_h in zip(hs, s, eg, k_col, vn, o):
            so_ref[0, h] = s_ * e + k * n
            o_ref[0, pl.ds(h, 1), :] = _rms(o_h, ng_ref[...]) * _silu(gate_ref[0, pl.ds(h, 1), :])
        return carry

    lax.fori_loop(0, heads // hb, head_block, 0)


def gdn_step(q, k, v, gate, beta, eg, norm_g, state):
    b, heads, dk, dv = state.shape
    tr = lambda a: jnp.swapaxes(a, 1, 2)
    col = pl.BlockSpec((1, dk, heads), lambda i: (i, 0, 0))
    hrow = pl.BlockSpec((1, heads, dv), lambda i: (i, 0, 0))
    sc = pl.BlockSpec((1, 1, heads), lambda i: (i, 0, 0))
    st = pl.BlockSpec((1, heads, dk, dv), lambda i: (i, 0, 0, 0))
    return pl.pallas_call(
        _gdn_step_kernel,
        out_shape=(jax.ShapeDtypeStruct((b, heads, dv), F32), jax.ShapeDtypeStruct(state.shape, F32)),
        grid=(b,),
        in_specs=[col, col, hrow, hrow, sc, sc, pl.BlockSpec((1, dv), lambda i: (0, 0)), st],
        out_specs=(hrow, st),
        compiler_params=_cparams(("parallel",)),
        name="gdn_step",
    )(tr(q), tr(k), v, gate, beta.reshape(b, 1, heads), eg.reshape(b, 1, heads),
      norm_g.reshape(1, dv), state)


def _hgrn_step_kernel(q_ref, k_ref, e_ref, v_ref, gate_ref, ng_ref, s_ref, o_ref, so_ref):
    heads = s_ref.shape[1]
    q_t, k_t, e_t = q_ref[0], k_ref[0], e_ref[0]
    lane = _iota((1, heads), 1)

    hb = _HEADS_PER_ITER
    col_sum = lambda x: jnp.sum(x, axis=0, keepdims=True)

    def head_block(i, carry):
        hs = [i * hb + k for k in range(hb)]
        s = [s_ref[0, h] for h in hs]
        onehot = [lane == h for h in hs]
        q_col = [_lane_pick(q_t, o) for o in onehot]
        k_col = [_lane_pick(k_t, o) for o in onehot]
        e_col = [_lane_pick(e_t, o) for o in onehot]
        v_row = [v_ref[0, pl.ds(h, 1), :] for h in hs]
        qk = [col_sum(q * k) for q, k in zip(q_col, k_col)]
        o = [a * v + col_sum((q * e) * s_) for a, v, q, e, s_ in zip(qk, v_row, q_col, e_col, s)]
        for h, s_, e, k, v, o_h in zip(hs, s, e_col, k_col, v_row, o):
            so_ref[0, h] = s_ * e + k * v
            o_ref[0, pl.ds(h, 1), :] = _rms(o_h, ng_ref[...]) * _silu(gate_ref[0, pl.ds(h, 1), :])
        return carry

    lax.fori_loop(0, heads // hb, head_block, 0)


def hgrn_step(q, k, e, v, gate, norm_g, state):
    b, heads, dk, dv = state.shape
    tr = lambda a: jnp.swapaxes(a, 1, 2)
    col = pl.BlockSpec((1, dk, heads), lambda i: (i, 0, 0))
    hrow = pl.BlockSpec((1, heads, dv), lambda i: (i, 0, 0))
    st = pl.BlockSpec((1, heads, dk, dv), lambda i: (i, 0, 0, 0))
    return pl.pallas_call(
        _hgrn_step_kernel,
        out_shape=(jax.ShapeDtypeStruct((b, heads, dv), F32), jax.ShapeDtypeStruct(state.shape, F32)),
        grid=(b,),
        in_specs=[col, col, col, hrow, hrow, pl.BlockSpec((1, dv), lambda i: (0, 0)), st],
        out_specs=(hrow, st),
        compiler_params=_cparams(("parallel",)),
        name="hgrn_step",
    )(tr(q), tr(k), tr(e), v, gate, norm_g.reshape(1, dv), state)


def _head_in_kernel(x_ref, w_ref, o_ref):
    o_ref[0] = _bdot_nt(x_ref[...], w_ref[...])


def _head_out_kernel(x_ref, w_ref, o_ref):
    o_ref[...] = _bdot(x_ref[0], w_ref[...]).astype(o_ref.dtype)


def latent_in(q_nope, w_kvb, j):
    b = q_nope.shape[0]
    r = w_kvb.shape[-2]
    return pl.pallas_call(
        _head_in_kernel,
        out_shape=jax.ShapeDtypeStruct((ML_HEADS, b, r), F32),
        grid=(ML_HEADS,),
        in_specs=[pl.BlockSpec((b, ML_NOPE), lambda h: (0, h)),
                  pl.BlockSpec((None, r, ML_NOPE), lambda h: (j, 0, 2 * h))],
        out_specs=pl.BlockSpec((1, b, r), lambda h: (h, 0, 0)),
        compiler_params=_cparams(("parallel",)),
        name="mla_latent_in",
    )(q_nope, w_kvb)


def latent_out(o_lat, w_kvb, j):
    _, b, r = o_lat.shape
    return pl.pallas_call(
        _head_out_kernel,
        out_shape=jax.ShapeDtypeStruct((b, ML_HEADS * ML_V), BF16),
        grid=(ML_HEADS,),
        in_specs=[pl.BlockSpec((1, b, r), lambda h: (h, 0, 0)),
                  pl.BlockSpec((None, r, ML_V), lambda h: (j, 0, 2 * h + 1))],
        out_specs=pl.BlockSpec((b, ML_V), lambda h: (0, h)),
        compiler_params=_cparams(("parallel",)),
        name="mla_latent_out",
    )(o_lat, w_kvb)


PAGES_PER_STEP = 32


def _decode_kernel(pt_ref, q_ref, new_ref, *rest, scale, kv_rank):
    cache_refs = rest[:PAGES_PER_STEP]
    o_ref, m_s, l_s, acc_s = rest[PAGES_PER_STEP:]
    pg = pl.program_id(1)
    q = q_ref[0]

    @pl.when(pg == 0)
    def _():
        new = new_ref[0]
        m_s[...] = jnp.sum(q.astype(F32) * new.astype(BF16).astype(F32), axis=1, keepdims=True) * scale
        l_s[...] = jnp.ones_like(l_s)
        acc_s[...] = jnp.broadcast_to(new[:, :kv_rank], acc_s.shape)

    rows = [c[...].astype(BF16) for c in cache_refs]
    scores = [_bdot(q, r) * scale for r in rows]
    m_old = m_s[...]
    m_new = m_old
    for s in scores:
        m_new = jnp.maximum(m_new, jnp.max(s, axis=1, keepdims=True))
    alpha = jnp.exp(m_old - m_new)
    l = l_s[...] * alpha
    acc = acc_s[...] * alpha
    for s, r in zip(scores, rows):
        p = jnp.exp(s - m_new)
        l = l + jnp.sum(p, axis=1, keepdims=True)
        acc = acc + _bdot_nt(p, r[:kv_rank, :])
    m_s[...] = m_new
    l_s[...] = l
    acc_s[...] = acc

    @pl.when(pg == pl.num_programs(1) - 1)
    def _():
        o_ref[0] = acc / l


def mla_decode(q_cat, new_rows, cache, j, page_table, kv_rank):
    b, heads, row = q_cat.shape
    n_pages = page_table.shape[1]
    page = cache.shape[3]
    assert n_pages % PAGES_PER_STEP == 0
    scale = (ML_NOPE + row - kv_rank) ** -0.5

    def cache_spec(i):
        return pl.BlockSpec((None, None, row, page),
                            lambda bi, pg, pt: (j, pt[bi, pg * PAGES_PER_STEP + i], 0, 0))

    grid_spec = pltpu.PrefetchScalarGridSpec(
        num_scalar_prefetch=1,
        grid=(b, n_pages // PAGES_PER_STEP),
        in_specs=[pl.BlockSpec((1, heads, row), lambda bi, pg, pt: (bi, 0, 0)),
                  pl.BlockSpec((1, 1, row), lambda bi, pg, pt: (bi, 0, 0))]
                 + [cache_spec(i) for i in range(PAGES_PER_STEP)],
        out_specs=pl.BlockSpec((1, heads, kv_rank), lambda bi, pg, pt: (bi, 0, 0)),
        scratch_shapes=[pltpu.VMEM((heads, 1), F32), pltpu.VMEM((heads, 1), F32),
                        pltpu.VMEM((heads, kv_rank), F32)],
    )
    return pl.pallas_call(
        functools.partial(_decode_kernel, scale=scale, kv_rank=kv_rank),
        out_shape=jax.ShapeDtypeStruct((b, heads, kv_rank), F32),
        grid_spec=grid_spec,
        compiler_params=_cparams(("parallel", "arbitrary")),
        name="mla_decode",
    )(page_table, q_cat, new_rows, *([cache] * PAGES_PER_STEP))


def mamba_sample(u, j, conv_buf, state, w_in, conv_w, conv_b, dt_bias, a_log, d_skip, norm_g, w_out):
    b, d = u.shape
    d_inner = norm_g.shape[-1]
    heads = dt_bias.shape[-1]
    conv_dim = conv_w.shape[-1]
    n = state.shape[-1]
    z = mm(u, w_in, n=d_inner, tn=512, lead=(j,))
    xbc = mm(u, w_in, n=conv_dim, tn=512, col_block=d_inner // 512, lead=(j,))
    dt_raw = mm(u, w_in, n=heads, tn=LANES, col_block=(d_inner + conv_dim) // LANES, lead=(j,))
    xc, new_buf = conv_step(xbc, conv_buf, conv_w[j], conv_b[j])
    xc = _silu(xc)
    xs = xc[:, :d_inner]
    bm = xc[:, d_inner:d_inner + MB_GROUPS * n].reshape(b, MB_GROUPS, n)
    cm = xc[:, d_inner + MB_GROUPS * n:].reshape(b, MB_GROUPS, n)
    y, new_state = ssd_step(xs, dt_raw, bm, cm, state, dt_bias[j], a_log[j], d_skip[j])
    yn = prenorm(y * _silu(z), norm_g[j])
    return mm(yn, w_out, n=d, tn=256, lead=(j,)), new_state, new_buf


def gdn_sample(u, j, conv_buf, state, w_in, conv_w, dt_bias, a_log, norm_g, w_out):
    b, d = u.shape
    _, heads, dk, _ = state.shape
    width = heads * dk
    qkvg = mm(u, w_in, n=4 * width, tn=512, lead=(j,))
    ba = mm(u, _tail_cols(w_in, 4 * width, j), n=LANES, tn=LANES)
    qkv, new_buf = conv_step(qkvg[:, :3 * width], conv_buf, conv_w[j])
    qkv = _silu(qkv).reshape(b, 3, heads, dk)
    q = _l2norm(qkv[:, 0]) * (dk ** -0.5)
    k = _l2norm(qkv[:, 1])
    beta = jax.nn.sigmoid(ba[:, :heads])
    eg = jnp.exp(-jnp.exp(a_log[j]) * _softplus(ba[:, heads:2 * heads] + dt_bias[j]))
    gate = qkvg[:, 3 * width:].reshape(b, heads, dk)
    o, new_state = gdn_step(q, k, qkv[:, 2], gate, beta, eg, norm_g[j], state)
    return mm(o.reshape(b, width), w_out, n=d, tn=512, lead=(j,)), new_state, new_buf


def hgrn_sample(u, j, layer, state, w_in, lb_logits, norm_g, w_out):
    b, d = u.shape
    _, heads, dk, dv = state.shape
    qfig = mm(u, w_in, n=4 * d, tn=512, lead=(j,)).reshape(b, 4, heads, dk)
    lb_p = jax.nn.softmax(lb_logits, axis=0)
    lb = (jnp.cumsum(lb_p, axis=0) - lb_p[0])[layer].reshape(heads, dk)
    fg = lb + (1.0 - lb) * jax.nn.sigmoid(qfig[:, 1])
    q = _silu(qfig[:, 0]) * (dk ** -0.5)
    o, new_state = hgrn_step(q, 1.0 - fg, jnp.exp(jnp.log(fg)), qfig[:, 2], qfig[:, 3], norm_g[j], state)
    return mm(o.reshape(b, d), w_out, n=d, tn=512, lead=(j,)), new_state


def mla_sample(u, j, past_len, cache, page_table, w_a, q_norm, kv_norm, w_qb, w_kvb, w_o):
    b, d = u.shape
    kv_rank = kv_norm.shape[-1]
    pos = jnp.full((1,), past_len, jnp.int32)
    q_nope, q_pe, ckv, k_pe, rope_dim = mla_project(u, pos, j, w_a, q_norm, kv_norm, w_qb)
    new_rows = jnp.concatenate([ckv, k_pe[:, :rope_dim]], axis=-1)
    q_lat = jnp.swapaxes(latent_in(q_nope, w_kvb, j), 0, 1)
    q_cat = jnp.concatenate([q_lat.astype(BF16), q_pe.reshape(b, ML_HEADS, rope_dim)], axis=-1)
    o_lat = mla_decode(q_cat, new_rows[:, None, :], jnp.swapaxes(cache, 2, 3), j, page_table, kv_rank)
    o = latent_out(jnp.swapaxes(o_lat, 0, 1), w_kvb, j)
    return mm(o, w_o, n=d, tn=512, lead=(j,)), new_rows[:, None, :]


N_MIXERS = 4


_STATE_KEYS = ("ssm", "ssm_conv", "gdn", "gdn_conv", "hgrn", "mla")


def _trunk(xp, xs, bp, tp, st, p):
    norm_g = p["norm_g"]
    depth = norm_g.shape[0]
    outs_p = {k: [] for k in _STATE_KEYS}
    outs_s = {k: [] for k in _STATE_KEYS}
    ffn_w = (p["ffn_w_gate"], p["ffn_w_up"], p["ffn_w_down"])
    hp, hs = xp, xs
    xnp, xns = prenorm(hp, norm_g[0, 0]), prenorm(hs, norm_g[0, 0])
    for i in range(depth):
        mix_id, j = i % N_MIXERS, i // N_MIXERS
        g = norm_g[i]
        yp, ys = ffn(xnp, xns, *ffn_w, i, 0)
        hp, up = sandwich(hp, yp, g[1], g[2], 0.5)
        hs, us = sandwich(hs, ys, g[1], g[2], 0.5)
        if mix_id == 0:
            args = (p["mb_w_in"], p["mb_conv_w"], p["mb_conv_b"], p["mb_dt_bias"], p["mb_a_log"],
                    p["mb_d"], p["mb_norm"], p["mb_w_out"])
            mix_p, s1, c1 = mamba_prompt(up, bp, tp, j, *args)
            outs_p["ssm"].append(s1)
            outs_p["ssm_conv"].append(c1)
            mix_s, s1, c1 = mamba_sample(us, j, st["ssm_conv"][j], st["ssm"][j], *args)
            outs_s["ssm"].append(s1)
            outs_s["ssm_conv"].append(c1)
        elif mix_id == 1:
            args = (p["gd_w_in"], p["gd_conv_w"], p["gd_dt_bias"], p["gd_a_log"], p["gd_norm"],
                    p["gd_w_out"])
            mix_p, s1, c1 = gdn_prompt_mixer(up, bp, tp, j, *args)
            outs_p["gdn"].append(s1)
            outs_p["gdn_conv"].append(c1)
            mix_s, s1, c1 = gdn_sample(us, j, st["gdn_conv"][j], st["gdn"][j], *args)
            outs_s["gdn"].append(s1)
            outs_s["gdn_conv"].append(c1)
        elif mix_id == 2:
            args = (p["hg_w_in"], p["hg_lb_logits"], p["hg_norm"], p["hg_w_out"])
            mix_p, s1 = hgrn_prompt_mixer(up, bp, tp, j, i, *args)
            outs_p["hgrn"].append(s1)
            mix_s, s1 = hgrn_sample(us, j, i, st["hgrn"][j], *args)
            outs_s["hgrn"].append(s1)
        else:
            args = (p["ml_w_a"], p["ml_q_norm"], p["ml_kv_norm"], p["ml_w_qb"], p["ml_w_kvb"],
                    p["ml_w_o"])
            mix_p, rows = mla_prompt_mixer(up, bp, tp, j, *args)
            outs_p["mla"].append(rows)
            mix_s, rows = mla_sample(us, j, st["past_len"], st["cache"], st["page_table"], *args)
            outs_s["mla"].append(rows)
        hp, xnp = sandwich(hp, mix_p, g[3], g[4], 1.0)
        hs, xns = sandwich(hs, mix_s, g[3], g[4], 1.0)
        yp, ys = ffn(xnp, xns, *ffn_w, i, 1)
        g_next = norm_g[i + 1, 0] if i + 1 < depth else None
        hp, xnp = sandwich(hp, yp, g[5], g_next, 0.5)
        hs, xns = sandwich(hs, ys, g[5], g_next, 0.5)
    stack = lambda outs: tuple(jnp.stack(outs[k]) for k in _STATE_KEYS)
    return (hp, hs) + stack(outs_p) + stack(outs_s)


def kernel(x_prompt, x_sample, state_ssm, state_ssm_conv, state_gdn, state_gdn_conv, state_hgrn, cache_mla, page_table, norm_g, ffn_w_gate, ffn_w_up, ffn_w_down, mb_w_in, mb_conv_w, mb_conv_b, mb_dt_bias, mb_a_log, mb_d, mb_norm, mb_w_out, gd_w_in, gd_conv_w, gd_dt_bias, gd_a_log, gd_norm, gd_w_out, hg_w_in, hg_lb_logits, hg_norm, hg_w_out, ml_w_a, ml_q_norm, ml_kv_norm, ml_w_qb, ml_w_kvb, ml_w_o):
    p = dict(norm_g=norm_g, ffn_w_gate=ffn_w_gate, ffn_w_up=ffn_w_up, ffn_w_down=ffn_w_down,
             mb_w_in=mb_w_in, mb_conv_w=mb_conv_w, mb_conv_b=mb_conv_b, mb_dt_bias=mb_dt_bias,
             mb_a_log=mb_a_log, mb_d=mb_d, mb_norm=mb_norm, mb_w_out=mb_w_out,
             gd_w_in=gd_w_in, gd_conv_w=gd_conv_w, gd_dt_bias=gd_dt_bias, gd_a_log=gd_a_log,
             gd_norm=gd_norm, gd_w_out=gd_w_out,
             hg_w_in=hg_w_in, hg_lb_logits=hg_lb_logits, hg_norm=hg_norm, hg_w_out=hg_w_out,
             ml_w_a=ml_w_a, ml_q_norm=ml_q_norm, ml_kv_norm=ml_kv_norm, ml_w_qb=ml_w_qb,
             ml_w_kvb=ml_w_kvb, ml_w_o=ml_w_o)
    bp, tp, d = x_prompt.shape
    bs, ts, _ = x_sample.shape
    assert ts == 1
    st = dict(ssm=state_ssm, ssm_conv=state_ssm_conv, gdn=state_gdn, gdn_conv=state_gdn_conv,
              hgrn=state_hgrn, cache=cache_mla, page_table=page_table,
              past_len=page_table.shape[1] * cache_mla.shape[2])
    out = _trunk(x_prompt.reshape(bp * tp, d), x_sample.reshape(bs * ts, d), bp, tp, st, p)
    return (out[0].reshape(bp, tp, d), out[1].reshape(bs, ts, d)) + out[2:]
```

```python
import functools
import math

import jax
import jax.numpy as jnp
from jax import lax
from jax.experimental import pallas as pl
from jax.experimental.pallas import tpu as pltpu

F32 = jnp.float32
BF16 = jnp.bfloat16
HIGHEST = lax.Precision.HIGHEST

LANES = 128
VMEM_LIMIT_BYTES = 60 * 1024 * 1024

NORM_EPS = 1e-6
CONV_W = 4
CHUNK = 64
HG_CHUNK = 32
MB_HEADDIM = 64
MB_GROUPS = 8
ML_HEADS = 32
ML_NOPE = 128
ML_V = 128
ML_ROPE_THETA = 10000.0


def _cparams(sem):
    return pltpu.CompilerParams(dimension_semantics=sem, vmem_limit_bytes=VMEM_LIMIT_BYTES)


def _silu(x):
    return x * jax.nn.sigmoid(x)


def _softplus(x):
    return jnp.maximum(x, 0.0) + jnp.log1p(jnp.exp(-jnp.abs(x)))


def _bdot(a, b):
    return jnp.dot(a.astype(BF16), b.astype(BF16), preferred_element_type=F32)


def _bdot_nt(a, b):
    return lax.dot_general(a.astype(BF16), b.astype(BF16), (((1,), (1,)), ((), ())),
                           preferred_element_type=F32)


def _fdot(a, b):
    return jnp.dot(a, b, preferred_element_type=F32, precision=HIGHEST)


def _rms(x, g):
    return x * lax.rsqrt(jnp.mean(x * x, axis=-1, keepdims=True) + NORM_EPS) * g


def _prenorm_kernel(x_ref, g_ref, o_ref):
    o_ref[...] = _rms(x_ref[...].astype(F32), g_ref[...]).astype(o_ref.dtype)


def prenorm(x, g, out_dtype=BF16, col_block=0, width=None, tm=256):
    m = x.shape[0]
    d = width or x.shape[1]
    tm = min(tm, m)
    return pl.pallas_call(
        _prenorm_kernel,
        out_shape=jax.ShapeDtypeStruct((m, d), out_dtype),
        grid=(m // tm,),
        in_specs=[pl.BlockSpec((tm, d), lambda i: (i, col_block)),
                  pl.BlockSpec((1, d), lambda i: (0, 0))],
        out_specs=pl.BlockSpec((tm, d), lambda i: (i, 0)),
        compiler_params=_cparams(("parallel",)),
        name="prenorm",
    )(x, g.reshape(1, d))


def _sandwich_kernel(h_ref, y_ref, gpost_ref, gpre_ref, hn_ref, x_ref, *, scale):
    hn = h_ref[...] + scale * _rms(y_ref[...], gpost_ref[...])
    hn_ref[...] = hn
    x_ref[...] = _rms(hn, gpre_ref[...]).astype(x_ref.dtype)


def _sandwich_last_kernel(h_ref, y_ref, gpost_ref, hn_ref, *, scale):
    hn_ref[...] = h_ref[...] + scale * _rms(y_ref[...], gpost_ref[...])


def sandwich(h, y, g_post, g_pre, scale, tm=128):
    m, d = h.shape
    tm = min(tm, m)
    row = pl.BlockSpec((tm, d), lambda i: (i, 0))
    vec = pl.BlockSpec((1, d), lambda i: (0, 0))
    if g_pre is None:
        return pl.pallas_call(
            functools.partial(_sandwich_last_kernel, scale=scale),
            out_shape=jax.ShapeDtypeStruct((m, d), F32),
            grid=(m // tm,), in_specs=[row, row, vec], out_specs=row,
            compiler_params=_cparams(("parallel",)), name="sandwich_last",
        )(h, y, g_post.reshape(1, d)), None
    return pl.pallas_call(
        functools.partial(_sandwich_kernel, scale=scale),
        out_shape=(jax.ShapeDtypeStruct((m, d), F32), jax.ShapeDtypeStruct((m, d), BF16)),
        grid=(m // tm,), in_specs=[row, row, vec, vec], out_specs=(row, row),
        compiler_params=_cparams(("parallel",)), name="sandwich",
    )(h, y, g_post.reshape(1, d), g_pre.reshape(1, d))


def _ffn_rows(x_ref, wg_ref, wu_ref, wd_ref, o_ref, j, kc, nc):
    d = x_ref.shape[1]
    g = None
    u = None
    for k0 in range(0, d, kc):
        xk = x_ref[:, k0:k0 + kc]
        pg = jnp.dot(xk, wg_ref[k0:k0 + kc, :].astype(BF16), preferred_element_type=F32)
        pu = jnp.dot(xk, wu_ref[k0:k0 + kc, :].astype(BF16), preferred_element_type=F32)
        g = pg if g is None else g + pg
        u = pu if u is None else u + pu
    hmid = (_silu(g) * u).astype(BF16)

    @pl.when(j == 0)
    def _():
        o_ref[...] = jnp.zeros_like(o_ref)

    for n0 in range(0, d, nc):
        o_ref[:, n0:n0 + nc] += jnp.dot(hmid, wd_ref[:, n0:n0 + nc].astype(BF16),
                                        preferred_element_type=F32)


def _ffn_kernel(x_ref, xs_ref, wg_ref, wu_ref, wd_ref, o_ref, os_ref, *, kc, nc):
    i, j = pl.program_id(0), pl.program_id(1)
    _ffn_rows(x_ref, wg_ref, wu_ref, wd_ref, o_ref, j, kc, nc)

    @pl.when(i == 0)
    def _():
        _ffn_rows(xs_ref, wg_ref, wu_ref, wd_ref, os_ref, j, kc, nc)


def ffn(x, xs, wg, wu, wd, layer, which, tm=1024, tf=256):
    m, d = x.shape
    ms = xs.shape[0]
    f = wg.shape[-1]
    tm = min(tm, m)
    kc = min(1024, d)
    nc = min(512, d)
    once = dict(pipeline_mode=pl.Buffered(1))
    return pl.pallas_call(
        functools.partial(_ffn_kernel, kc=kc, nc=nc),
        out_shape=(jax.ShapeDtypeStruct((m, d), F32), jax.ShapeDtypeStruct((ms, d), F32)),
        grid=(m // tm, f // tf),
        in_specs=[
            pl.BlockSpec((tm, d), lambda i, j: (i, 0), **once),
            pl.BlockSpec((ms, d), lambda i, j: (0, 0), **once),
            pl.BlockSpec((None, None, d, tf), lambda i, j: (layer, which, 0, j)),
            pl.BlockSpec((None, None, d, tf), lambda i, j: (layer, which, 0, j)),
            pl.BlockSpec((None, None, tf, d), lambda i, j: (layer, which, j, 0)),
        ],
        out_specs=(pl.BlockSpec((tm, d), lambda i, j: (i, 0), **once),
                   pl.BlockSpec((ms, d), lambda i, j: (0, 0), **once)),
        compiler_params=_cparams(("arbitrary", "arbitrary")),
        name="ffn",
    )(x, xs, wg, wu, wd)


def _mm_kernel(x_ref, w_ref, o_ref, *, kc):
    k = x_ref.shape[1]
    acc = None
    for k0 in range(0, k, kc):
        p = jnp.dot(x_ref[:, k0:k0 + kc].astype(BF16), w_ref[k0:k0 + kc, :].astype(BF16),
                    preferred_element_type=F32)
        acc = p if acc is None else acc + p
    o_ref[...] = acc.astype(o_ref.dtype)


def mm(x, w, *, n, tn, col_block=0, tm=1024, out_dtype=F32, lead=None):
    m, k = x.shape
    tm = min(tm, m)
    lead = tuple(lead or ())
    assert w.ndim == 2 + len(lead) and w.shape[-2] == k and n % tn == 0 and m % tm == 0
    assert (col_block * tn + n) <= w.shape[-1]
    wspec = pl.BlockSpec((None,) * len(lead) + (k, tn), lambda i, j: lead + (0, j + col_block))
    return pl.pallas_call(
        functools.partial(_mm_kernel, kc=min(1024, k)),
        out_shape=jax.ShapeDtypeStruct((m, n), out_dtype),
        grid=(m // tm, n // tn),
        in_specs=[pl.BlockSpec((tm, k), lambda i, j: (i, 0)), wspec],
        out_specs=pl.BlockSpec((tm, tn), lambda i, j: (i, j)),
        compiler_params=_cparams(("parallel", "arbitrary")),
        name="mm",
    )(x, w)


def _iota(shape, dim):
    return lax.broadcasted_iota(jnp.int32, shape, dim)


def _conv_silu(x, w_ref, b_ref=None):
    row = _iota(x.shape, 0)
    acc = x * w_ref[CONV_W - 1:CONV_W, :]
    for k in range(1, CONV_W):
        shifted = jnp.where(row >= k, pltpu.roll(x, k, 0), 0.0)
        acc = acc + shifted * w_ref[CONV_W - 1 - k:CONV_W - k, :]
    if b_ref is not None:
        acc = acc + b_ref[...]
    return _silu(acc)


def _chunk_cumsum(x, chunk):
    pos = _iota(x.shape, 0) & (chunk - 1)
    k = 1
    while k < chunk:
        x = x + jnp.where(pos >= k, pltpu.roll(x, k, 0), 0.0)
        k *= 2
    return x


def _lane_select_matrix(sel):
    return (_iota((LANES, LANES), 0) == sel).astype(BF16)


def _split2(a):
    hi = a.astype(BF16)
    return hi, (a - hi.astype(F32)).astype(BF16)


def _expand(x, pick):
    h1 = x.astype(BF16)
    r1 = x - h1.astype(F32)
    h2 = r1.astype(BF16)
    h3 = (r1 - h2.astype(F32)).astype(BF16)
    dot = functools.partial(jnp.dot, preferred_element_type=F32)
    return dot(h1, pick) + dot(h2, pick) + dot(h3, pick)


def _dot3(a_parts, b_parts):
    (ah, al), (bh, bl) = a_parts, b_parts
    dot = functools.partial(jnp.dot, preferred_element_type=F32)
    return dot(ah, bh) + (dot(al, bh) + dot(ah, bl))


def _ssd_kernel(xs_ref, bm_ref, cm_ref, wx_ref, wb_ref, wc_ref, bx_ref, bb_ref, bc_ref,
                dt_ref, dtb_ref, alog_ref, dsk_ref, z_ref, y_ref, st_ref,
                xs_s, bm_s, cm_s, cum_s, dte_s, dt_s, la_s, *, pairs_per_group):
    hp = pl.program_id(1)
    t_len = xs_ref.shape[1]
    half = LANES // 2
    xs_s[...] = _conv_silu(xs_ref[0], wx_ref, bx_ref)

    @pl.when(hp % pairs_per_group == 0)
    def _():
        bm_s[...] = _conv_silu(bm_ref[0], wb_ref, bb_ref)
        cm_s[...] = _conv_silu(cm_ref[0], wc_ref, bc_ref)

    @pl.when(hp == 0)
    def _():
        dt = _softplus(dt_ref[0] + dtb_ref[...])
        dt_s[...] = dt
        la_s[...] = dt * (-jnp.exp(alog_ref[...]))

    lane_row = _iota((1, LANES), 1)
    pick = _lane_select_matrix(2 * hp + (lane_row >= half).astype(jnp.int32))
    dte_s[...] = _expand(dt_s[...], pick)
    cum_s[...] = _chunk_cumsum(_expand(la_s[...], pick), CHUNK)

    t_io = _iota((CHUNK, LANES), 0)
    s_io = _iota((CHUNK, LANES), 1) & (half - 1)
    causal = s_io <= t_io
    same_head = (_iota((LANES, LANES), 0) >= half) == (_iota((LANES, LANES), 1) >= half)
    low = lane_row < half
    zeros = jnp.zeros((CHUNK, LANES), F32)
    dskip = dsk_ref[...]

    blk = 2 * CHUNK
    n_blocks = t_len // blk
    group = 4 if n_blocks % 4 == 0 else 1

    def each(f, *lists):
        return [f(*args) for args in zip(*lists)]

    def row_form(m_t, cc):
        a0, a1 = m_t[0:1, :], m_t[half:half + 1, :]
        if cc == 0:
            return jnp.where(low, a0, pltpu.roll(a1, half, 1))
        return jnp.where(low, pltpu.roll(a0, half, 1), a1)

    def blocks(i, st):
        rows = [pl.ds(pl.multiple_of((group * i + b_i) * blk, blk), blk) for b_i in range(group)]
        x2, b2, c2, cu2, dt2 = ([ref[r, :] for r in rows] for ref in (xs_s, bm_s, cm_s, cum_s, dte_s))
        cu_t, dt_t, b_t = (each(lambda a: a.T, l) for l in (cu2, dt2, b2))
        ccs = [cc for _ in range(group) for cc in range(2)]
        sl = [slice(cc * CHUNK, (cc + 1) * CHUNK) for cc in ccs]
        per_chunk = lambda l: [l[n // 2][sl[n]] for n in range(2 * group)]
        xc, bc, cmc, ce, de = (per_chunk(l) for l in (x2, b2, c2, cu2, dt2))
        cum_row = [row_form(cu_t[n // 2], ccs[n]) for n in range(2 * group)]
        dt_row = [row_form(dt_t[n // 2], ccs[n]) for n in range(2 * group)]
        cb = each(lambda c, b: _bdot_nt(c, jnp.concatenate([b, b], axis=0)), cmc, bc)
        wp = each(lambda cb_, ce_, cr, dr: cb_ * jnp.exp(jnp.where(causal, ce_ - cr, -jnp.inf)) * dr,
                  cb, ce, cum_row, dt_row)
        bd = each(lambda x: jnp.where(same_head, jnp.concatenate([x, x], axis=0), 0.0), xc)
        y_intra = each(_bdot, wp, bd)
        cl = each(lambda ce_: ce_[CHUNK - 1:CHUNK, :], ce)
        tx = each(lambda cl_, ce_, de_, x: jnp.exp(cl_ - ce_) * de_ * x, cl, ce, de, xc)
        tx_pad = [jnp.concatenate([t_, zeros] if cc == 0 else [zeros, t_], axis=0)
                  for t_, cc in zip(tx, ccs)]
        upd = [_bdot(b_t[n // 2], tx_pad[n]) for n in range(2 * group)]
        dec = each(jnp.exp, cl)
        starts = []
        for n in range(2 * group):
            starts.append(st)
            st = st * dec[n] + upd[n]
        y_inter = each(_bdot, cmc, starts)
        y = each(lambda yi, ce_, yn, x: yi + jnp.exp(ce_) * yn + x * dskip, y_intra, ce, y_inter, xc)
        for b_i, r in enumerate(rows):
            y_ref[0, r, :] = jnp.concatenate(y[2 * b_i:2 * b_i + 2], axis=0) * _silu(z_ref[0, r, :])
        return st

    st = lax.fori_loop(0, n_blocks // group, blocks, jnp.zeros((LANES, LANES), F32))
    st_ref[0] = st.T


def ssd_prompt(xbc, dt_raw, z, conv_w, conv_b, dt_bias, a_log, d_skip):
    b, t, conv_dim = xbc.shape
    d_inner = z.shape[-1]
    heads = d_inner // MB_HEADDIM
    n = (conv_dim - d_inner) // (2 * MB_GROUPS)
    assert heads == LANES and n == LANES and t % (2 * CHUNK) == 0
    pairs = heads // 2
    ppg = pairs // MB_GROUPS
    xoff, boff = 0, d_inner // LANES
    coff = boff + MB_GROUPS

    def seq(off, grouped):
        if grouped:
            return lambda bi, hp: (bi, 0, off + hp // ppg)
        return lambda bi, hp: (bi, 0, off + hp)

    def par(off, grouped):
        if grouped:
            return lambda bi, hp: (0, off + hp // ppg)
        return lambda bi, hp: (0, off + hp)

    slab = lambda f: pl.BlockSpec((1, t, LANES), f)
    wsp = lambda f: pl.BlockSpec((CONV_W, LANES), f)
    vsp = lambda f: pl.BlockSpec((1, LANES), f)
    cb2 = conv_b.reshape(1, conv_dim)
    dsk = jnp.repeat(d_skip, MB_HEADDIM).reshape(1, d_inner)
    return pl.pallas_call(
        functools.partial(_ssd_kernel, pairs_per_group=ppg),
        out_shape=(jax.ShapeDtypeStruct((b, t, d_inner), F32),
                   jax.ShapeDtypeStruct((b, d_inner, n), F32)),
        grid=(b, pairs),
        in_specs=[slab(seq(xoff, False)), slab(seq(boff, True)), slab(seq(coff, True)),
                  wsp(par(xoff, False)), wsp(par(boff, True)), wsp(par(coff, True)),
                  vsp(par(xoff, False)), vsp(par(boff, True)), vsp(par(coff, True)),
                  slab(lambda bi, hp: (bi, 0, 0)), vsp(lambda bi, hp: (0, 0)), vsp(lambda bi, hp: (0, 0)),
                  vsp(lambda bi, hp: (0, hp)), slab(lambda bi, hp: (bi, 0, hp))],
        out_specs=(slab(lambda bi, hp: (bi, 0, hp)),
                   pl.BlockSpec((1, LANES, LANES), lambda bi, hp: (bi, hp, 0))),
        scratch_shapes=[pltpu.VMEM((t, LANES), F32)] * 7,
        compiler_params=_cparams(("arbitrary", "arbitrary")),
        name="ssd_prompt",
    )(xbc, xbc, xbc, conv_w, conv_w, conv_w, cb2, cb2, cb2,
      dt_raw, dt_bias.reshape(1, heads), a_log.reshape(1, heads), dsk, z)


def mamba_prompt(u, b, t, j, w_in, conv_w, conv_b, dt_bias, a_log, d_skip, norm_g, w_out):
    m, d = u.shape
    d_inner = norm_g.shape[-1]
    heads = dt_bias.shape[-1]
    conv_dim = conv_w.shape[-1]
    z = mm(u, w_in, n=d_inner, tn=512, lead=(j,))
    xbc = mm(u, w_in, n=conv_dim, tn=512, col_block=d_inner // 512, lead=(j,))
    dt_raw = mm(u, w_in, n=heads, tn=LANES, col_block=(d_inner + conv_dim) // LANES, lead=(j,))
    xbc3 = xbc.reshape(b, t, conv_dim)
    gated, state = ssd_prompt(xbc3, dt_raw.reshape(b, t, heads), z.reshape(b, t, d_inner),
                              conv_w[j], conv_b[j], dt_bias[j], a_log[j], d_skip[j])
    yn = prenorm(gated.reshape(m, d_inner), norm_g[j], tm=128)
    mix = mm(yn, w_out, n=d, tn=256, tm=1024, lead=(j,))
    n = state.shape[-1]
    return mix, state.reshape(b, heads, MB_HEADDIM, n), xbc3[:, t - (CONV_W - 1):, :]


def _l2norm(x):
    return x * lax.rsqrt(jnp.sum(x * x, axis=-1, keepdims=True) + NORM_EPS)


def _gdn_kernel(q_ref, k_ref, v_ref, gate_ref, wq_ref, wk_ref, wv_ref, ba_ref, dtb_ref, alog_ref,
                g_ref, o_ref, st_ref, q_s, k_s, v_s, beta_s, cum_s, u_s, w_s, a_s, *, heads, hps):
    hg = pl.program_id(1)
    t_len = q_ref.shape[1]
    dk = q_ref.shape[2] // hps
    ba = ba_ref[0]
    beta_all = jax.nn.sigmoid(ba)
    lg_all = -jnp.exp(alog_ref[...]) * _softplus(ba + dtb_ref[...])
    for hh in range(hps):
        cols = slice(hh * dk, (hh + 1) * dk)
        h = hg * hps + hh
        q_s[hh] = _l2norm(_conv_silu(q_ref[0, :, cols], wq_ref[:, cols])) * (dk ** -0.5)
        k_s[hh] = _l2norm(_conv_silu(k_ref[0, :, cols], wk_ref[:, cols]))
        v_s[hh] = _conv_silu(v_ref[0, :, cols], wv_ref[:, cols])
        beta_s[hh] = _expand(beta_all, _lane_select_matrix(h))
        cum_s[hh] = _chunk_cumsum(_expand(lg_all, _lane_select_matrix(heads + h)), CHUNK)

    blk = 2 * CHUNK
    r_io, c_io = _iota((blk, blk), 0), _iota((blk, blk), 1)
    same_chunk = (r_io >= CHUNK) == (c_io >= CHUNK)
    strict = same_chunk & (c_io < r_io)
    incl = same_chunk & (c_io <= r_io)
    eye = (r_io == c_io).astype(F32)
    lane_row = _iota((1, blk), 1)
    zeros = jnp.zeros((CHUNK, dk), F32)

    n_blocks = t_len // blk
    group = 4 if n_blocks % 4 == 0 else 1

    def each(f, *lists):
        return [f(*args) for args in zip(*lists)]

    def solve_blocks(hh, i, carry):
        rows = [pl.ds(pl.multiple_of((group * i + b_i) * blk, blk), blk) for b_i in range(group)]
        q2, k2, v2 = ([ref[hh, r, :] for r in rows] for ref in (q_s, k_s, v_s))
        be2, g2 = ([ref[hh, r, :] for r in rows] for ref in (beta_s, cum_s))
        diff = each(lambda g: g - g.T[0:1, :], g2)
        n1 = each(lambda be, k, d: -(be * _bdot_nt(k, k) * jnp.exp(jnp.where(strict, d, -jnp.inf))),
                  be2, k2, diff)
        square = lambda s: _dot3(s, s)
        pair = lambda a, b, sa, sb: eye + a + b + _dot3(sa, sb)
        s1 = each(_split2, n1)
        n2 = each(square, s1)
        s2 = each(_split2, n2)
        n4 = each(square, s2)
        f12 = each(pair, n1, n2, s1, s2)
        s4 = each(_split2, n4)
        n8 = each(square, s4)
        s8 = each(_split2, n8)
        n16 = each(square, s8)
        f48 = each(pair, n4, n8, s4, s8)
        s16 = each(_split2, n16)
        n32 = each(square, s16)
        f1632 = each(pair, n16, n32, s16, each(_split2, n32))
        f1248 = each(_dot3, each(_split2, f12), each(_split2, f48))
        t_inv = each(_dot3, each(_split2, f1248), each(_split2, f1632))
        rhs = each(lambda v, k, be, g: jnp.concatenate([v * be, k * (be * jnp.exp(g))], axis=1),
                   v2, k2, be2, g2)
        uw = each(_dot3, each(_split2, t_inv), each(_split2, rhs))
        attn = each(lambda q, k, d: _bdot_nt(q, k) * jnp.exp(jnp.where(incl, d, -jnp.inf)),
                    q2, k2, diff)
        for r, uw_b, a_b in zip(rows, uw, attn):
            u_s[hh, r, :] = uw_b[:, :dk]
            w_s[hh, r, :] = uw_b[:, dk:]
            a_s[hh, r, :] = a_b
        return carry

    for hh in range(hps):
        lax.fori_loop(0, n_blocks // group, functools.partial(solve_blocks, hh), 0)

    def pair_of_chunks(i, sts):
        rows = pl.ds(pl.multiple_of(i * blk, blk), blk)
        hs = list(range(hps))
        sts = list(sts)
        q2, k2, g2 = ([ref[hh, rows, :] for hh in hs] for ref in (q_s, k_s, cum_s))
        u2, w2 = ([ref[hh, rows, :] for hh in hs] for ref in (u_s, w_s))
        g_row = each(lambda g: g.T[0:1, :], g2)
        k_t = each(lambda k: k.T, k2)
        qe2 = each(lambda q, g: q * jnp.exp(g), q2, g2)
        vns = [[] for _ in hs]
        o_inter = [[] for _ in hs]
        for cc in range(2):
            sl = slice(cc * CHUNK, (cc + 1) * CHUNK)
            in_chunk = (lane_row >= cc * CHUNK) & (lane_row < (cc + 1) * CHUNK)
            vn = [u[sl] - _bdot(w[sl], st) for u, w, st in zip(u2, w2, sts)]
            for hh in hs:
                o_inter[hh].append(_bdot(qe2[hh][sl], sts[hh]))
            g_last = [g[(cc + 1) * CHUNK - 1:(cc + 1) * CHUNK, :] for g in g2]
            kd = [kt * jnp.where(in_chunk, jnp.exp(gl - gr), 0.0)
                  for kt, gl, gr in zip(k_t, g_last, g_row)]
            vn_pad = [jnp.concatenate([v, zeros] if cc == 0 else [zeros, v], axis=0) for v in vn]
            sts = [st * jnp.exp(gl) + _bdot(kd_, vp) for st, gl, kd_, vp in zip(sts, g_last, kd, vn_pad)]
            for hh in hs:
                vns[hh].append(vn[hh])
        for hh in hs:
            cols = slice(hh * dk, (hh + 1) * dk)
            o2 = (jnp.concatenate(o_inter[hh], axis=0)
                  + _bdot(a_s[hh, rows, :], jnp.concatenate(vns[hh], axis=0)))
            o_ref[0, rows, cols] = (_rms(o2, g_ref[...])
                                    * _silu(gate_ref[0, rows, cols])).astype(o_ref.dtype)
        return tuple(sts)

    sts = lax.fori_loop(0, t_len // blk, pair_of_chunks,
                        tuple(jnp.zeros((dk, dk), F32) for _ in range(hps)))
    for hh in range(hps):
        st_ref[0, hh] = sts[hh]


GDN_HEADS_PER_STEP = 2


def gdn_prompt(qkvg, ba, conv_w, dt_bias, a_log, norm_g):
    b, t, w4 = qkvg.shape
    width = w4 // 4
    dk = norm_g.shape[-1]
    heads = width // dk
    hps = GDN_HEADS_PER_STEP
    assert dk == LANES and 2 * heads <= LANES and t % (2 * CHUNK) == 0 and heads % hps == 0
    pad = LANES - 2 * heads
    dtb = jnp.pad(dt_bias, (heads, pad)).reshape(1, LANES)
    alog = jnp.pad(a_log, (heads, pad)).reshape(1, LANES)
    hgs = heads // hps
    slab = lambda seg: pl.BlockSpec((1, t, hps * dk), lambda bi, h: (bi, 0, seg * hgs + h))
    wsp = lambda seg: pl.BlockSpec((CONV_W, hps * dk), lambda bi, h: (0, seg * hgs + h))
    vec = pl.BlockSpec((1, LANES), lambda bi, h: (0, 0))
    return pl.pallas_call(
        functools.partial(_gdn_kernel, heads=heads, hps=hps),
        out_shape=(jax.ShapeDtypeStruct((b, t, width), BF16),
                   jax.ShapeDtypeStruct((b, heads, dk, dk), F32)),
        grid=(b, hgs),
        in_specs=[slab(0), slab(1), slab(2), slab(3), wsp(0), wsp(1), wsp(2),
                  pl.BlockSpec((1, t, LANES), lambda bi, h: (bi, 0, 0)), vec, vec, vec],
        out_specs=(pl.BlockSpec((1, t, hps * dk), lambda bi, h: (bi, 0, h)),
                   pl.BlockSpec((1, hps, dk, dk), lambda bi, h: (bi, h, 0, 0))),
        scratch_shapes=[pltpu.VMEM((hps, t, dk), F32)] * 8,
        compiler_params=_cparams(("parallel", "arbitrary")),
        name="gdn_prompt",
    )(qkvg, qkvg, qkvg, qkvg, conv_w, conv_w, conv_w, ba, dtb, alog, norm_g.reshape(1, dk))


def _tail_cols(w, start, lead):
    cols = w[lead][:, start:]
    return jnp.pad(cols, ((0, 0), (0, LANES - cols.shape[1])))


def gdn_prompt_mixer(u, b, t, j, w_in, conv_w, dt_bias, a_log, norm_g, w_out):
    m, d = u.shape
    width = conv_w.shape[-1] // 3
    qkvg = mm(u, w_in, n=4 * width, tn=512, lead=(j,))
    ba = mm(u, _tail_cols(w_in, 4 * width, j), n=LANES, tn=LANES)
    o, state = gdn_prompt(qkvg.reshape(b, t, 4 * width), ba.reshape(b, t, LANES), conv_w[j],
                          dt_bias[j], a_log[j], norm_g[j])
    mix = mm(o.reshape(m, width), w_out, n=d, tn=512, lead=(j,))
    new_buf = qkvg.reshape(b, t, 4 * width)[:, t - (CONV_W - 1):, :3 * width]
    return mix, state, new_buf


def _hgrn_lower_bound(lb_ref, layer):
    logits = lb_ref[...]
    e = jnp.exp(logits - jnp.max(logits, axis=0, keepdims=True))
    p = e / jnp.sum(e, axis=0, keepdims=True)
    lb = jnp.zeros((1, logits.shape[1]), F32)
    for r in range(1, layer + 1):
        lb = lb + p[r:r + 1, :]
    return lb


def _hgrn_kernel(q_ref, f_ref, i_ref, g_ref, lb_ref, ng_ref, o_ref, st_ref,
                 q_s, k_s, cum_s, *, layer):
    t_len = q_ref.shape[1]
    dk = q_ref.shape[2]
    lb = _hgrn_lower_bound(lb_ref, layer)
    fg = lb + (1.0 - lb) * jax.nn.sigmoid(f_ref[0])
    q_s[...] = _silu(q_ref[0]) * (dk ** -0.5)
    k_s[...] = 1.0 - fg
    cum_s[...] = _chunk_cumsum(jnp.log(fg), HG_CHUNK)

    blk = 4 * HG_CHUNK
    sub = 8
    row_blk = _iota((blk, dk), 0)
    r_io, c_io = _iota((blk, blk), 0), _iota((blk, blk), 1)
    leaf_base = r_io & ~(sub - 1)

    n_blocks = t_len // blk
    group = 2 if n_blocks % 2 == 0 else 1
    per_blk = blk // HG_CHUNK

    def each(f, *lists):
        return [f(*args) for args in zip(*lists)]

    def intra_attention(r0, q4, k4, cu4):
        attn = jnp.zeros((blk, blk), F32)
        for j in range(sub):
            pick = lambda a: jnp.broadcast_to(a.reshape(blk // sub, sub, dk)[:, j:j + 1, :],
                                              (blk // sub, sub, dk)).reshape(blk, dk)
            ok = (row_blk & (sub - 1)) >= j
            col = jnp.sum(q4 * pick(k4) * jnp.exp(jnp.where(ok, cu4 - pick(cu4), -jnp.inf)),
                          axis=1, keepdims=True)
            attn = jnp.where(c_io == leaf_base + j, col, attn)
        for size in (2 * sub, 4 * sub):
            mid = jnp.concatenate(
                [jnp.broadcast_to(cum_s[pl.ds(r0 + g * size + size // 2, 1), :], (size, dk))
                 for g in range(blk // size)], axis=0)
            upper = (row_blk & (size - 1)) >= size // 2
            qf = q4 * jnp.exp(jnp.where(upper, cu4 - mid, -jnp.inf))
            kf = k4 * jnp.exp(jnp.where(upper, -jnp.inf, mid - cu4))
            same = (r_io & ~(size - 1)) == (c_io & ~(size - 1))
            attn = attn + jnp.where(same, _bdot_nt(qf, kf), 0.0)
        return attn

    def blocks(i, st_t):
        r0s = [pl.multiple_of((group * i + b_i) * blk, blk) for b_i in range(group)]
        rows = [pl.ds(r0, blk) for r0 in r0s]
        q4, k4, cu4 = ([ref[r, :] for r in rows] for ref in (q_s, k_s, cum_s))
        v4 = [i_ref[0, r, :] for r in rows]
        v4_t = each(lambda v: v.T, v4)
        o_intra = each(_bdot, each(intra_attention, r0s, q4, k4, cu4), v4)
        qe4 = each(lambda q, cu: q * jnp.exp(cu), q4, cu4)
        chunks = [(b_i, cc * HG_CHUNK) for b_i in range(group) for cc in range(per_blk)]
        last = [cu4[b_i][c0 + HG_CHUNK - 1:c0 + HG_CHUNK, :] for b_i, c0 in chunks]
        upd = [_bdot(v4_t[b_i],
                     jnp.where((row_blk >= c0) & (row_blk < c0 + HG_CHUNK),
                               k4[b_i] * jnp.exp(l - cu4[b_i]), 0.0))
               for (b_i, c0), l in zip(chunks, last)]
        dec = each(jnp.exp, last)
        starts = []
        for n in range(len(chunks)):
            starts.append(st_t)
            st_t = st_t * dec[n] + upd[n]
        o_inter = [_bdot_nt(qe4[b_i][c0:c0 + HG_CHUNK], s) for (b_i, c0), s in zip(chunks, starts)]
        for b_i, r in enumerate(rows):
            o4 = o_intra[b_i] + jnp.concatenate(o_inter[b_i * per_blk:(b_i + 1) * per_blk], axis=0)
            o_ref[0, r, :] = (_rms(o4, ng_ref[...]) * _silu(g_ref[0, r, :])).astype(o_ref.dtype)
        return st_t

    st_t = lax.fori_loop(0, n_blocks // group, blocks, jnp.zeros((dk, dk), F32))
    st_ref[0, 0] = st_t.T


def hgrn_prompt(qfig, lb_logits, norm_g, layer):
    b, t, d4 = qfig.shape
    d = d4 // 4
    dk = norm_g.shape[-1]
    heads = d // dk
    depth = lb_logits.shape[0]
    assert dk == LANES and t % (4 * HG_CHUNK) == 0
    slab = lambda off: pl.BlockSpec((1, t, dk), lambda bi, h: (bi, 0, off + h))
    return pl.pallas_call(
        functools.partial(_hgrn_kernel, layer=layer),
        out_shape=(jax.ShapeDtypeStruct((b, t, d), BF16),
                   jax.ShapeDtypeStruct((b, heads, dk, dk), F32)),
        grid=(b, heads),
        in_specs=[slab(0), slab(heads), slab(2 * heads), slab(3 * heads),
                  pl.BlockSpec((depth, dk), lambda bi, h: (0, h)),
                  pl.BlockSpec((1, dk), lambda bi, h: (0, 0))],
        out_specs=(pl.BlockSpec((1, t, dk), lambda bi, h: (bi, 0, h)),
                   pl.BlockSpec((1, 1, dk, dk), lambda bi, h: (bi, h, 0, 0))),
        scratch_shapes=[pltpu.VMEM((t, dk), F32)] * 3,
        compiler_params=_cparams(("parallel", "arbitrary")),
        name="hgrn_prompt",
    )(qfig, qfig, qfig, qfig, lb_logits, norm_g.reshape(1, dk))


def hgrn_prompt_mixer(u, b, t, j, layer, w_in, lb_logits, norm_g, w_out):
    m, d = u.shape
    qfig = mm(u, w_in, n=4 * d, tn=512, lead=(j,))
    o, state = hgrn_prompt(qfig.reshape(b, t, 4 * d), lb_logits, norm_g[j], layer)
    mix = mm(o.reshape(m, d), w_out, n=d, tn=512, lead=(j,))
    return mix, state


def rope_tables(pos, rope_dim):
    half = rope_dim // 2
    inv = ML_ROPE_THETA ** (-jnp.arange(half, dtype=F32) / half)
    ang = pos.astype(F32)[:, None] * inv
    cos, sin = jnp.cos(ang), jnp.sin(ang)
    reps = LANES // rope_dim
    return (jnp.tile(jnp.concatenate([cos, cos], axis=-1), (1, reps)),
            jnp.tile(jnp.concatenate([-sin, sin], axis=-1), (1, reps)))


def _rope_kernel(x_ref, cos_ref, sin_ref, o_ref, *, rope_dim, dup):
    half = rope_dim // 2
    lane = _iota((x_ref.shape[0], LANES), 1)
    first = (lane & (rope_dim - 1)) < half
    cos, sin = cos_ref[...], sin_ref[...]
    for c in range(x_ref.shape[1] // LANES):
        x = x_ref[:, c * LANES:(c + 1) * LANES]
        other = jnp.where(first, pltpu.roll(x, LANES - half, 1), pltpu.roll(x, half, 1))
        y = x * cos + other * sin
        if dup:
            y = y + pltpu.roll(y, rope_dim, 1)
        o_ref[:, c * LANES:(c + 1) * LANES] = y.astype(o_ref.dtype)


def rope(x, cos, sin, rope_dim, out_dtype, dup=False, tm=256):
    m, w = x.shape
    t = cos.shape[0]
    if t == 1:
        tm = min(tm, m)
        tab = pl.BlockSpec((1, LANES), lambda i: (0, 0))
    else:
        tm = min(tm, t)
        nt = t // tm
        tab = pl.BlockSpec((tm, LANES), lambda i: (i % nt, 0))
    return pl.pallas_call(
        functools.partial(_rope_kernel, rope_dim=rope_dim, dup=dup),
        out_shape=jax.ShapeDtypeStruct((m, w), out_dtype),
        grid=(m // tm,),
        in_specs=[pl.BlockSpec((tm, w), lambda i: (i, 0)), tab, tab],
        out_specs=pl.BlockSpec((tm, w), lambda i: (i, 0)),
        compiler_params=_cparams(("parallel",)),
        name="rope",
    )(x, cos, sin)


def _mla_attn_kernel(qn_ref, qp_ref, kn_ref, kp_ref, v_ref, o_ref, *, tq, scale, rope_dim):
    h = pl.program_id(1)
    t_len = qn_ref.shape[1]
    lane = _iota((tq, LANES), 1)
    mine = (lane // rope_dim) == (h % (LANES // rope_dim))
    for q0 in range(0, t_len, tq):
        kv_len = q0 + tq
        qn = qn_ref[0, q0:q0 + tq, :]
        qp = jnp.where(mine, qp_ref[0, q0:q0 + tq, :], 0.0)
        s = (_bdot_nt(qn, kn_ref[0, :kv_len, :]) + _bdot_nt(qp, kp_ref[0, :kv_len, :])) * scale
        ok = _iota((tq, kv_len), 1) <= _iota((tq, kv_len), 0) + q0
        s = jnp.where(ok, s, -jnp.inf)
        p = jnp.exp(s - jnp.max(s, axis=-1, keepdims=True))
        l = jnp.sum(p, axis=-1, keepdims=True)
        o_ref[0, q0:q0 + tq, :] = (_bdot(p, v_ref[0, :kv_len, :]) / l).astype(o_ref.dtype)


def mla_prompt_attention(qn, qp, kv, kp, rope_dim):
    b, t, w = qn.shape
    heads = w // ML_NOPE
    tq = min(512, t)
    scale = (ML_NOPE + rope_dim) ** -0.5
    per_tile = LANES // rope_dim
    slab = lambda f: pl.BlockSpec((1, t, LANES), f)
    return pl.pallas_call(
        functools.partial(_mla_attn_kernel, tq=tq, scale=scale, rope_dim=rope_dim),
        out_shape=jax.ShapeDtypeStruct((b, t, heads * ML_V), BF16),
        grid=(b, heads),
        in_specs=[slab(lambda bi, h: (bi, 0, h)), slab(lambda bi, h: (bi, 0, h // per_tile)),
                  slab(lambda bi, h: (bi, 0, 2 * h)), slab(lambda bi, h: (bi, 0, 0)),
                  slab(lambda bi, h: (bi, 0, 2 * h + 1))],
        out_specs=slab(lambda bi, h: (bi, 0, h)),
        compiler_params=_cparams(("parallel", "arbitrary")),
        name="mla_attn",
    )(qn, qp, kv, kp, kv)


def mla_project(u, pos, j, w_a, q_norm, kv_norm, w_qb):
    q_rank, kv_rank = q_norm.shape[-1], kv_norm.shape[-1]
    rope_dim = w_a.shape[-1] - q_rank - kv_rank
    a1 = mm(u, w_a, n=q_rank + kv_rank, tn=512, lead=(j,))
    kpe_raw = mm(u, _tail_cols(w_a, q_rank + kv_rank, j), n=LANES, tn=LANES)
    q_c = prenorm(a1, q_norm[j], width=q_rank)
    ckv = prenorm(a1, kv_norm[j], out_dtype=F32, col_block=q_rank // kv_rank, width=kv_rank)
    cos, sin = rope_tables(pos, rope_dim)
    k_pe = rope(kpe_raw, cos, sin, rope_dim, F32, dup=True)
    wq = w_qb[j].reshape(q_rank, ML_HEADS, ML_NOPE + rope_dim)
    w_nope = wq[:, :, :ML_NOPE].reshape(q_rank, ML_HEADS * ML_NOPE)
    w_pe = wq[:, :, ML_NOPE:].reshape(q_rank, ML_HEADS * rope_dim)
    q_nope = mm(q_c, w_nope, n=ML_HEADS * ML_NOPE, tn=512, out_dtype=BF16)
    q_pe = rope(mm(q_c, w_pe, n=ML_HEADS * rope_dim, tn=512), cos, sin, rope_dim, BF16)
    return q_nope, q_pe, ckv, k_pe, rope_dim


def mla_prompt_mixer(u, b, t, j, w_a, q_norm, kv_norm, w_qb, w_kvb, w_o):
    m, d = u.shape
    q_nope, q_pe, ckv, k_pe, rope_dim = mla_project(u, jnp.arange(t), j, w_a, q_norm, kv_norm, w_qb)
    kv = mm(ckv, w_kvb, n=w_kvb.shape[-1], tn=512, out_dtype=BF16, lead=(j,))
    o = mla_prompt_attention(q_nope.reshape(b, t, -1), q_pe.reshape(b, t, -1), kv.reshape(b, t, -1),
                             k_pe.astype(BF16).reshape(b, t, LANES), rope_dim)
    mix = mm(o.reshape(m, -1), w_o, n=d, tn=512, lead=(j,))
    rows = jnp.concatenate([ckv, k_pe[:, :rope_dim]], axis=-1).reshape(b, t, -1)
    return mix, rows


def _lane_pick(x, onehot):
    return jnp.sum(jnp.where(onehot, x, 0.0), axis=1, keepdims=True)


_HEADS_PER_ITER_SSD = 8
_HEADS_PER_ITER = 4


def conv_step(x, buf, w, b=None):
    xp = jnp.concatenate([buf, x[:, None, :]], axis=1)
    y = xp[:, 0] * w[0]
    for k in range(1, CONV_W):
        y = y + xp[:, k] * w[k]
    if b is not None:
        y = y + b
    return y, xp[:, 1:]


def _ssd_step_kernel(xs_ref, dt_ref, dtb_ref, alog_ref, dsk_ref, bm_ref, cm_ref, s_ref,
                     y_ref, so_ref, *, heads_per_group):
    heads = s_ref.shape[1]
    xs_t = xs_ref[0]
    dt = _softplus(dt_ref[0] + dtb_ref[...])
    dec = jnp.exp(dt * (-jnp.exp(alog_ref[...])))
    dtxs = xs_t * dt
    lane = _iota((1, heads), 1)
    cb = jnp.zeros((1, heads), F32)
    for g in range(bm_ref.shape[1]):
        cb_g = jnp.sum(cm_ref[0, g:g + 1, :] * bm_ref[0, g:g + 1, :], axis=1, keepdims=True)
        cb = jnp.where(lane // heads_per_group == g, cb_g, cb)

    hb = _HEADS_PER_ITER_SSD
    assert heads_per_group % hb == 0

    def head_block(i, y_acc):
        hs = [i * hb + k for k in range(hb)]
        g = (i * hb) // heads_per_group
        cm_row, bm_row = cm_ref[0, pl.ds(g, 1), :], bm_ref[0, pl.ds(g, 1), :]
        s = [s_ref[0, h] for h in hs]
        onehot = [lane == h for h in hs]
        dec_h = [_lane_pick(dec, o) for o in onehot]
        dx_col = [_lane_pick(dtxs, o) for o in onehot]
        y_col = [jnp.sum(s_ * cm_row, axis=1, keepdims=True) * d for s_, d in zip(s, dec_h)]
        for h, s_, d, x in zip(hs, s, dec_h, dx_col):
            so_ref[0, h] = s_ * d + x * bm_row
        for o, y in zip(onehot, y_col):
            y_acc = jnp.where(o, y, y_acc)
        return y_acc

    y_acc = lax.fori_loop(0, heads // hb, head_block, jnp.zeros(xs_t.shape, F32))
    y_ref[0] = y_acc + cb * dtxs + xs_t * dsk_ref[...]


def ssd_step(xs, dt_raw, bm, cm, state, dt_bias, a_log, d_skip):
    b, heads, p, n = state.shape
    groups = bm.shape[1]
    xs_t = jnp.swapaxes(xs.reshape(b, heads, p), 1, 2)
    row = lambda a: a.reshape(1, heads)
    vec = pl.BlockSpec((1, heads), lambda i: (0, 0))
    y_t, new_state = pl.pallas_call(
        functools.partial(_ssd_step_kernel, heads_per_group=heads // groups),
        out_shape=(jax.ShapeDtypeStruct((b, p, heads), F32), jax.ShapeDtypeStruct(state.shape, F32)),
        grid=(b,),
        in_specs=[pl.BlockSpec((1, p, heads), lambda i: (i, 0, 0)),
                  pl.BlockSpec((1, 1, heads), lambda i: (i, 0, 0)), vec, vec, vec,
                  pl.BlockSpec((1, groups, n), lambda i: (i, 0, 0)),
                  pl.BlockSpec((1, groups, n), lambda i: (i, 0, 0)),
                  pl.BlockSpec((1, heads, p, n), lambda i: (i, 0, 0, 0))],
        out_specs=(pl.BlockSpec((1, p, heads), lambda i: (i, 0, 0)),
                   pl.BlockSpec((1, heads, p, n), lambda i: (i, 0, 0, 0))),
        compiler_params=_cparams(("parallel",)),
        name="ssd_step",
    )(xs_t, dt_raw.reshape(b, 1, heads), row(dt_bias), row(a_log), row(d_skip), bm, cm, state)
    return jnp.swapaxes(y_t, 1, 2).reshape(b, heads * p), new_state


def _gdn_step_kernel(q_ref, k_ref, v_ref, gate_ref, beta_ref, eg_ref, ng_ref, s_ref, o_ref, so_ref):
    heads = s_ref.shape[1]
    q_t, k_t = q_ref[0], k_ref[0]
    lane = _iota((1, heads), 1)

    hb = _HEADS_PER_ITER
    col_sum = lambda x: jnp.sum(x, axis=0, keepdims=True)

    def head_block(i, carry):
        hs = [i * hb + k for k in range(hb)]
        s = [s_ref[0, h] for h in hs]
        onehot = [lane == h for h in hs]
        q_col = [_lane_pick(q_t, o) for o in onehot]
        k_col = [_lane_pick(k_t, o) for o in onehot]
        beta = [_lane_pick(beta_ref[0], o) for o in onehot]
        eg = [_lane_pick(eg_ref[0], o) for o in onehot]
        v_row = [v_ref[0, pl.ds(h, 1), :] for h in hs]
        vn = [v * b - col_sum((k * (b * e)) * s_) for v, b, e, k, s_ in zip(v_row, beta, eg, k_col, s)]
        qk = [col_sum(q * k) for q, k in zip(q_col, k_col)]
        o = [e * col_sum(q * s_) + a * n for e, q, s_, a, n in zip(eg, q_col, s, qk, vn)]
        for h, s_, e, k, n, o_h in zip(hs, s, eg, k_col, vn, o):
            so_ref[0, h] = s_ * e + k * n
            o_ref[0, pl.ds(h, 1), :] = _rms(o_h, ng_ref[...]) * _silu(gate_ref[0, pl.ds(h, 1), :])
        return carry

    lax.fori_loop(0, heads // hb, head_block, 0)


def gdn_step(q, k, v, gate, beta, eg, norm_g, state):
    b, heads, dk, dv = state.shape
    tr = lambda a: jnp.swapaxes(a, 1, 2)
    col = pl.BlockSpec((1, dk, heads), lambda i: (i, 0, 0))
    hrow = pl.BlockSpec((1, heads, dv), lambda i: (i, 0, 0))
    sc = pl.BlockSpec((1, 1, heads), lambda i: (i, 0, 0))
    st = pl.BlockSpec((1, heads, dk, dv), lambda i: (i, 0, 0, 0))
    return pl.pallas_call(
        _gdn_step_kernel,
        out_shape=(jax.ShapeDtypeStruct((b, heads, dv), F32), jax.ShapeDtypeStruct(state.shape, F32)),
        grid=(b,),
        in_specs=[col, col, hrow, hrow, sc, sc, pl.BlockSpec((1, dv), lambda i: (0, 0)), st],
        out_specs=(hrow, st),
        compiler_params=_cparams(("parallel",)),
        name="gdn_step",
    )(tr(q), tr(k), v, gate, beta.reshape(b, 1, heads), eg.reshape(b, 1, heads),
      norm_g.reshape(1, dv), state)


def _hgrn_step_kernel(q_ref, k_ref, e_ref, v_ref, gate_ref, ng_ref, s_ref, o_ref, so_ref):
    heads = s_ref.shape[1]
    q_t, k_t, e_t = q_ref[0], k_ref[0], e_ref[0]
    lane = _iota((1, heads), 1)

    hb = _HEADS_PER_ITER
    col_sum = lambda x: jnp.sum(x, axis=0, keepdims=True)

    def head_block(i, carry):
        hs = [i * hb + k for k in range(hb)]
        s = [s_ref[0, h] for h in hs]
        onehot = [lane == h for h in hs]
        q_col = [_lane_pick(q_t, o) for o in onehot]
        k_col = [_lane_pick(k_t, o) for o in onehot]
        e_col = [_lane_pick(e_t, o) for o in onehot]
        v_row = [v_ref[0, pl.ds(h, 1), :] for h in hs]
        qk = [col_sum(q * k) for q, k in zip(q_col, k_col)]
        o = [a * v + col_sum((q * e) * s_) for a, v, q, e, s_ in zip(qk, v_row, q_col, e_col, s)]
        for h, s_, e, k, v, o_h in zip(hs, s, e_col, k_col, v_row, o):
            so_ref[0, h] = s_ * e + k * v
            o_ref[0, pl.ds(h, 1), :] = _rms(o_h, ng_ref[...]) * _silu(gate_ref[0, pl.ds(h, 1), :])
        return carry

    lax.fori_loop(0, heads // hb, head_block, 0)


def hgrn_step(q, k, e, v, gate, norm_g, state):
    b, heads, dk, dv = state.shape
    tr = lambda a: jnp.swapaxes(a, 1, 2)
    col = pl.BlockSpec((1, dk, heads), lambda i: (i, 0, 0))
    hrow = pl.BlockSpec((1, heads, dv), lambda i: (i, 0, 0))
    st = pl.BlockSpec((1, heads, dk, dv), lambda i: (i, 0, 0, 0))
    return pl.pallas_call(
        _hgrn_step_kernel,
        out_shape=(jax.ShapeDtypeStruct((b, heads, dv), F32), jax.ShapeDtypeStruct(state.shape, F32)),
        grid=(b,),
        in_specs=[col, col, col, hrow, hrow, pl.BlockSpec((1, dv), lambda i: (0, 0)), st],
        out_specs=(hrow, st),
        compiler_params=_cparams(("parallel",)),
        name="hgrn_step",
    )(tr(q), tr(k), tr(e), v, gate, norm_g.reshape(1, dv), state)


def _head_in_kernel(x_ref, w_ref, o_ref):
    o_ref[0] = _bdot_nt(x_ref[...], w_ref[...])


def _head_out_kernel(x_ref, w_ref, o_ref):
    o_ref[...] = _bdot(x_ref[0], w_ref[...]).astype(o_ref.dtype)


def latent_in(q_nope, w_kvb, j):
    b = q_nope.shape[0]
    r = w_kvb.shape[-2]
    return pl.pallas_call(
        _head_in_kernel,
        out_shape=jax.ShapeDtypeStruct((ML_HEADS, b, r), F32),
        grid=(ML_HEADS,),
        in_specs=[pl.BlockSpec((b, ML_NOPE), lambda h: (0, h)),
                  pl.BlockSpec((None, r, ML_NOPE), lambda h: (j, 0, 2 * h))],
        out_specs=pl.BlockSpec((1, b, r), lambda h: (h, 0, 0)),
        compiler_params=_cparams(("parallel",)),
        name="mla_latent_in",
    )(q_nope, w_kvb)


def latent_out(o_lat, w_kvb, j):
    _, b, r = o_lat.shape
    return pl.pallas_call(
        _head_out_kernel,
        out_shape=jax.ShapeDtypeStruct((b, ML_HEADS * ML_V), BF16),
        grid=(ML_HEADS,),
        in_specs=[pl.BlockSpec((1, b, r), lambda h: (h, 0, 0)),
                  pl.BlockSpec((None, r, ML_V), lambda h: (j, 0, 2 * h + 1))],
        out_specs=pl.BlockSpec((b, ML_V), lambda h: (0, h)),
        compiler_params=_cparams(("parallel",)),
        name="mla_latent_out",
    )(o_lat, w_kvb)


PAGES_PER_STEP = 32


def _decode_kernel(pt_ref, q_ref, new_ref, *rest, scale, kv_rank):
    cache_refs = rest[:PAGES_PER_STEP]
    o_ref, m_s, l_s, acc_s = rest[PAGES_PER_STEP:]
    pg = pl.program_id(1)
    q = q_ref[0]

    @pl.when(pg == 0)
    def _():
        new = new_ref[0]
        m_s[...] = jnp.sum(q.astype(F32) * new.astype(BF16).astype(F32), axis=1, keepdims=True) * scale
        l_s[...] = jnp.ones_like(l_s)
        acc_s[...] = jnp.broadcast_to(new[:, :kv_rank], acc_s.shape)

    rows = [c[...].astype(BF16) for c in cache_refs]
    scores = [_bdot(q, r) * scale for r in rows]
    m_old = m_s[...]
    m_new = m_old
    for s in scores:
        m_new = jnp.maximum(m_new, jnp.max(s, axis=1, keepdims=True))
    alpha = jnp.exp(m_old - m_new)
    l = l_s[...] * alpha
    acc = acc_s[...] * alpha
    for s, r in zip(scores, rows):
        p = jnp.exp(s - m_new)
        l = l + jnp.sum(p, axis=1, keepdims=True)
        acc = acc + _bdot_nt(p, r[:kv_rank, :])
    m_s[...] = m_new
    l_s[...] = l
    acc_s[...] = acc

    @pl.when(pg == pl.num_programs(1) - 1)
    def _():
        o_ref[0] = acc / l


def mla_decode(q_cat, new_rows, cache, j, page_table, kv_rank):
    b, heads, row = q_cat.shape
    n_pages = page_table.shape[1]
    page = cache.shape[3]
    assert n_pages % PAGES_PER_STEP == 0
    scale = (ML_NOPE + row - kv_rank) ** -0.5

    def cache_spec(i):
        return pl.BlockSpec((None, None, row, page),
                            lambda bi, pg, pt: (j, pt[bi, pg * PAGES_PER_STEP + i], 0, 0))

    grid_spec = pltpu.PrefetchScalarGridSpec(
        num_scalar_prefetch=1,
        grid=(b, n_pages // PAGES_PER_STEP),
        in_specs=[pl.BlockSpec((1, heads, row), lambda bi, pg, pt: (bi, 0, 0)),
                  pl.BlockSpec((1, 1, row), lambda bi, pg, pt: (bi, 0, 0))]
                 + [cache_spec(i) for i in range(PAGES_PER_STEP)],
        out_specs=pl.BlockSpec((1, heads, kv_rank), lambda bi, pg, pt: (bi, 0, 0)),
        scratch_shapes=[pltpu.VMEM((heads, 1), F32), pltpu.VMEM((heads, 1), F32),
                        pltpu.VMEM((heads, kv_rank), F32)],
    )
    return pl.pallas_call(
        functools.partial(_decode_kernel, scale=scale, kv_rank=kv_rank),
        out_shape=jax.ShapeDtypeStruct((b, heads, kv_rank), F32),
        grid_spec=grid_spec,
        compiler_params=_cparams(("parallel", "arbitrary")),
        name="mla_decode",
    )(page_table, q_cat, new_rows, *([cache] * PAGES_PER_STEP))


def mamba_sample(u, j, conv_buf, state, w_in, conv_w, conv_b, dt_bias, a_log, d_skip, norm_g, w_out):
    b, d = u.shape
    d_inner = norm_g.shape[-1]
    heads = dt_bias.shape[-1]
    conv_dim = conv_w.shape[-1]
    n = state.shape[-1]
    z = mm(u, w_in, n=d_inner, tn=512, lead=(j,))
    xbc = mm(u, w_in, n=conv_dim, tn=512, col_block=d_inner // 512, lead=(j,))
    dt_raw = mm(u, w_in, n=heads, tn=LANES, col_block=(d_inner + conv_dim) // LANES, lead=(j,))
    xc, new_buf = conv_step(xbc, conv_buf, conv_w[j], conv_b[j])
    xc = _silu(xc)
    xs = xc[:, :d_inner]
    bm = xc[:, d_inner:d_inner + MB_GROUPS * n].reshape(b, MB_GROUPS, n)
    cm = xc[:, d_inner + MB_GROUPS * n:].reshape(b, MB_GROUPS, n)
    y, new_state = ssd_step(xs, dt_raw, bm, cm, state, dt_bias[j], a_log[j], d_skip[j])
    yn = prenorm(y * _silu(z), norm_g[j])
    return mm(yn, w_out, n=d, tn=256, lead=(j,)), new_state, new_buf


def gdn_sample(u, j, conv_buf, state, w_in, conv_w, dt_bias, a_log, norm_g, w_out):
    b, d = u.shape
    _, heads, dk, _ = state.shape
    width = heads * dk
    qkvg = mm(u, w_in, n=4 * width, tn=512, lead=(j,))
    ba = mm(u, _tail_cols(w_in, 4 * width, j), n=LANES, tn=LANES)
    qkv, new_buf = conv_step(qkvg[:, :3 * width], conv_buf, conv_w[j])
    qkv = _silu(qkv).reshape(b, 3, heads, dk)
    q = _l2norm(qkv[:, 0]) * (dk ** -0.5)
    k = _l2norm(qkv[:, 1])
    beta = jax.nn.sigmoid(ba[:, :heads])
    eg = jnp.exp(-jnp.exp(a_log[j]) * _softplus(ba[:, heads:2 * heads] + dt_bias[j]))
    gate = qkvg[:, 3 * width:].reshape(b, heads, dk)
    o, new_state = gdn_step(q, k, qkv[:, 2], gate, beta, eg, norm_g[j], state)
    return mm(o.reshape(b, width), w_out, n=d, tn=512, lead=(j,)), new_state, new_buf


def hgrn_sample(u, j, layer, state, w_in, lb_logits, norm_g, w_out):
    b, d = u.shape
    _, heads, dk, dv = state.shape
    qfig = mm(u, w_in, n=4 * d, tn=512, lead=(j,)).reshape(b, 4, heads, dk)
    lb_p = jax.nn.softmax(lb_logits, axis=0)
    lb = (jnp.cumsum(lb_p, axis=0) - lb_p[0])[layer].reshape(heads, dk)
    fg = lb + (1.0 - lb) * jax.nn.sigmoid(qfig[:, 1])
    q = _silu(qfig[:, 0]) * (dk ** -0.5)
    o, new_state = hgrn_step(q, 1.0 - fg, jnp.exp(jnp.log(fg)), qfig[:, 2], qfig[:, 3], norm_g[j], state)
    return mm(o.reshape(b, d), w_out, n=d, tn=512, lead=(j,)), new_state


def mla_sample(u, j, past_len, cache, page_table, w_a, q_norm, kv_norm, w_qb, w_kvb, w_o):
    b, d = u.shape
    kv_rank = kv_norm.shape[-1]
    pos = jnp.full((1,), past_len, jnp.int32)
    q_nope, q_pe, ckv, k_pe, rope_dim = mla_project(u, pos, j, w_a, q_norm, kv_norm, w_qb)
    new_rows = jnp.concatenate([ckv, k_pe[:, :rope_dim]], axis=-1)
    q_lat = jnp.swapaxes(latent_in(q_nope, w_kvb, j), 0, 1)
    q_cat = jnp.concatenate([q_lat.astype(BF16), q_pe.reshape(b, ML_HEADS, rope_dim)], axis=-1)
    o_lat = mla_decode(q_cat, new_rows[:, None, :], jnp.swapaxes(cache, 2, 3), j, page_table, kv_rank)
    o = latent_out(jnp.swapaxes(o_lat, 0, 1), w_kvb, j)
    return mm(o, w_o, n=d, tn=512, lead=(j,)), new_rows[:, None, :]


N_MIXERS = 4


_STATE_KEYS = ("ssm", "ssm_conv", "gdn", "gdn_conv", "hgrn", "mla")


def _trunk(xp, xs, bp, tp, st, p):
    norm_g = p["norm_g"]
    depth = norm_g.shape[0]
    outs_p = {k: [] for k in _STATE_KEYS}
    outs_s = {k: [] for k in _STATE_KEYS}
    ffn_w = (p["ffn_w_gate"], p["ffn_w_up"], p["ffn_w_down"])
    hp, hs = xp, xs
    xnp, xns = prenorm(hp, norm_g[0, 0]), prenorm(hs, norm_g[0, 0])
    for i in range(depth):
        mix_id, j = i % N_MIXERS, i // N_MIXERS
        g = norm_g[i]
        yp, ys = ffn(xnp, xns, *ffn_w, i, 0)
        hp, up = sandwich(hp, yp, g[1], g[2], 0.5)
        hs, us = sandwich(hs, ys, g[1], g[2], 0.5)
        if mix_id == 0:
            args = (p["mb_w_in"], p["mb_conv_w"], p["mb_conv_b"], p["mb_dt_bias"], p["mb_a_log"],
                    p["mb_d"], p["mb_norm"], p["mb_w_out"])
            mix_p, s1, c1 = mamba_prompt(up, bp, tp, j, *args)
            outs_p["ssm"].append(s1)
            outs_p["ssm_conv"].append(c1)
            mix_s, s1, c1 = mamba_sample(us, j, st["ssm_conv"][j], st["ssm"][j], *args)
            outs_s["ssm"].append(s1)
            outs_s["ssm_conv"].append(c1)
        elif mix_id == 1:
            args = (p["gd_w_in"], p["gd_conv_w"], p["gd_dt_bias"], p["gd_a_log"], p["gd_norm"],
                    p["gd_w_out"])
            mix_p, s1, c1 = gdn_prompt_mixer(up, bp, tp, j, *args)
            outs_p["gdn"].append(s1)
            outs_p["gdn_conv"].append(c1)
            mix_s, s1, c1 = gdn_sample(us, j, st["gdn_conv"][j], st["gdn"][j], *args)
            outs_s["gdn"].append(s1)
            outs_s["gdn_conv"].append(c1)
        elif mix_id == 2:
            args = (p["hg_w_in"], p["hg_lb_logits"], p["hg_norm"], p["hg_w_out"])
            mix_p, s1 = hgrn_prompt_mixer(up, bp, tp, j, i, *args)
            outs_p["hgrn"].append(s1)
            mix_s, s1 = hgrn_sample(us, j, i, st["hgrn"][j], *args)
            outs_s["hgrn"].append(s1)
        else:
            args = (p["ml_w_a"], p["ml_q_norm"], p["ml_kv_norm"], p["ml_w_qb"], p["ml_w_kvb"],
                    p["ml_w_o"])
            mix_p, rows = mla_prompt_mixer(up, bp, tp, j, *args)
            outs_p["mla"].append(rows)
            mix_s, rows = mla_sample(us, j, st["past_len"], st["cache"], st["page_table"], *args)
            outs_s["mla"].append(rows)
        hp, xnp = sandwich(hp, mix_p, g[3], g[4], 1.0)
        hs, xns = sandwich(hs, mix_s, g[3], g[4], 1.0)
        yp, ys = ffn(xnp, xns, *ffn_w, i, 1)
        g_next = norm_g[i + 1, 0] if i + 1 < depth else None
        hp, xnp = sandwich(hp, yp, g[5], g_next, 0.5)
        hs, xns = sandwich(hs, ys, g[5], g_next, 0.5)
    stack = lambda outs: tuple(jnp.stack(outs[k]) for k in _STATE_KEYS)
    return (hp, hs) + stack(outs_p) + stack(outs_s)


def kernel(x_prompt, x_sample, state_ssm, state_ssm_conv, state_gdn, state_gdn_conv, state_hgrn, cache_mla, page_table, norm_g, ffn_w_gate, ffn_w_up, ffn_w_down, mb_w_in, mb_conv_w, mb_conv_b, mb_dt_bias, mb_a_log, mb_d, mb_norm, mb_w_out, gd_w_in, gd_conv_w, gd_dt_bias, gd_a_log, gd_norm, gd_w_out, hg_w_in, hg_lb_logits, hg_norm, hg_w_out, ml_w_a, ml_q_norm, ml_kv_norm, ml_w_qb, ml_w_kvb, ml_w_o):
    p = dict(norm_g=norm_g, ffn_w_gate=ffn_w_gate, ffn_w_up=ffn_w_up, ffn_w_down=ffn_w_down,
             mb_w_in=mb_w_in, mb_conv_w=mb_conv_w, mb_conv_b=mb_conv_b, mb_dt_bias=mb_dt_bias,
             mb_a_log=mb_a_log, mb_d=mb_d, mb_norm=mb_norm, mb_w_out=mb_w_out,
             gd_w_in=gd_w_in, gd_conv_w=gd_conv_w, gd_dt_bias=gd_dt_bias, gd_a_log=gd_a_log,
             gd_norm=gd_norm, gd_w_out=gd_w_out,
             hg_w_in=hg_w_in, hg_lb_logits=hg_lb_logits, hg_norm=hg_norm, hg_w_out=hg_w_out,
             ml_w_a=ml_w_a, ml_q_norm=ml_q_norm, ml_kv_norm=ml_kv_norm, ml_w_qb=ml_w_qb,
             ml_w_kvb=ml_w_kvb, ml_w_o=ml_w_o)
    bp, tp, d = x_prompt.shape
    bs, ts, _ = x_sample.shape
    assert ts == 1
    st = dict(ssm=state_ssm, ssm_conv=state_ssm_conv, gdn=state_gdn, gdn_conv=state_gdn_conv,
              hgrn=state_hgrn, cache=cache_mla, page_table=page_table,
              past_len=page_table.shape[1] * cache_mla.shape[2])
    out = _trunk(x_prompt.reshape(bp * tp, d), x_sample.reshape(bs * ts, d), bp, tp, st, p)
    return (out[0].reshape(bp, tp, d), out[1].reshape(bs, ts, d)) + out[2:]
```

```python
import functools
import math

import jax
import jax.numpy as jnp
from jax import lax
from jax.experimental import pallas as pl
from jax.experimental.pallas import tpu as pltpu

F32 = jnp.float32
BF16 = jnp.bfloat16
HIGHEST = lax.Precision.HIGHEST

LANES = 128
VMEM_LIMIT_BYTES = 60 * 1024 * 1024

NORM_EPS = 1e-6
CONV_W = 4
CHUNK = 64
HG_CHUNK = 32
MB_HEADDIM = 64
MB_GROUPS = 8
ML_HEADS = 32
ML_NOPE = 128
ML_V = 128
ML_ROPE_THETA = 10000.0


def _cparams(sem):
    return pltpu.CompilerParams(dimension_semantics=sem, vmem_limit_bytes=VMEM_LIMIT_BYTES)


def _silu(x):
    return x * jax.nn.sigmoid(x)


def _softplus(x):
    return jnp.maximum(x, 0.0) + jnp.log1p(jnp.exp(-jnp.abs(x)))


def _bdot(a, b):
    return jnp.dot(a.astype(BF16), b.astype(BF16), preferred_element_type=F32)


def _bdot_nt(a, b):
    return lax.dot_general(a.astype(BF16), b.astype(BF16), (((1,), (1,)), ((), ())),
                           preferred_element_type=F32)


def _fdot(a, b):
    return jnp.dot(a, b, preferred_element_type=F32, precision=HIGHEST)


def _rms(x, g):
    return x * lax.rsqrt(jnp.mean(x * x, axis=-1, keepdims=True) + NORM_EPS) * g


def _prenorm_kernel(x_ref, g_ref, o_ref):
    o_ref[...] = _rms(x_ref[...].astype(F32), g_ref[...]).astype(o_ref.dtype)


def prenorm(x, g, out_dtype=BF16, col_block=0, width=None, tm=256):
    m = x.shape[0]
    d = width or x.shape[1]
    tm = min(tm, m)
    return pl.pallas_call(
        _prenorm_kernel,
        out_shape=jax.ShapeDtypeStruct((m, d), out_dtype),
        grid=(m // tm,),
        in_specs=[pl.BlockSpec((tm, d), lambda i: (i, col_block)),
                  pl.BlockSpec((1, d), lambda i: (0, 0))],
        out_specs=pl.BlockSpec((tm, d), lambda i: (i, 0)),
        compiler_params=_cparams(("parallel",)),
        name="prenorm",
    )(x, g.reshape(1, d))


def _sandwich_kernel(h_ref, y_ref, gpost_ref, gpre_ref, hn_ref, x_ref, *, scale):
    hn = h_ref[...] + scale * _rms(y_ref[...], gpost_ref[...])
    hn_ref[...] = hn
    x_ref[...] = _rms(hn, gpre_ref[...]).astype(x_ref.dtype)


def _sandwich_last_kernel(h_ref, y_ref, gpost_ref, hn_ref, *, scale):
    hn_ref[...] = h_ref[...] + scale * _rms(y_ref[...], gpost_ref[...])


def sandwich(h, y, g_post, g_pre, scale, tm=128):
    m, d = h.shape
    tm = min(tm, m)
    row = pl.BlockSpec((tm, d), lambda i: (i, 0))
    vec = pl.BlockSpec((1, d), lambda i: (0, 0))
    if g_pre is None:
        return pl.pallas_call(
            functools.partial(_sandwich_last_kernel, scale=scale),
            out_shape=jax.ShapeDtypeStruct((m, d), F32),
            grid=(m // tm,), in_specs=[row, row, vec], out_specs=row,
            compiler_params=_cparams(("parallel",)), name="sandwich_last",
        )(h, y, g_post.reshape(1, d)), None
    return pl.pallas_call(
        functools.partial(_sandwich_kernel, scale=scale),
        out_shape=(jax.ShapeDtypeStruct((m, d), F32), jax.ShapeDtypeStruct((m, d), BF16)),
        grid=(m // tm,), in_specs=[row, row, vec, vec], out_specs=(row, row),
        compiler_params=_cparams(("parallel",)), name="sandwich",
    )(h, y, g_post.reshape(1, d), g_pre.reshape(1, d))


def _ffn_rows(x_ref, wg_ref, wu_ref, wd_ref, o_ref, j, kc, nc):
    d = x_ref.shape[1]
    g = None
    u = None
    for k0 in range(0, d, kc):
        xk = x_ref[:, k0:k0 + kc]
        pg = jnp.dot(xk, wg_ref[k0:k0 + kc, :].astype(BF16), preferred_element_type=F32)
        pu = jnp.dot(xk, wu_ref[k0:k0 + kc, :].astype(BF16), preferred_element_type=F32)
        g = pg if g is None else g + pg
        u = pu if u is None else u + pu
    hmid = (_silu(g) * u).astype(BF16)

    @pl.when(j == 0)
    def _():
        o_ref[...] = jnp.zeros_like(o_ref)

    for n0 in range(0, d, nc):
        o_ref[:, n0:n0 + nc] += jnp.dot(hmid, wd_ref[:, n0:n0 + nc].astype(BF16),
                                        preferred_element_type=F32)


def _ffn_kernel(x_ref, xs_ref, wg_ref, wu_ref, wd_ref, o_ref, os_ref, *, kc, nc):
    i, j = pl.program_id(0), pl.program_id(1)
    _ffn_rows(x_ref, wg_ref, wu_ref, wd_ref, o_ref, j, kc, nc)

    @pl.when(i == 0)
    def _():
        _ffn_rows(xs_ref, wg_ref, wu_ref, wd_ref, os_ref, j, kc, nc)


def ffn(x, xs, wg, wu, wd, layer, which, tm=1024, tf=256):
    m, d = x.shape
    ms = xs.shape[0]
    f = wg.shape[-1]
    tm = min(tm, m)
    kc = min(1024, d)
    nc = min(512, d)
    once = dict(pipeline_mode=pl.Buffered(1))
    return pl.pallas_call(
        functools.partial(_ffn_kernel, kc=kc, nc=nc),
        out_shape=(jax.ShapeDtypeStruct((m, d), F32), jax.ShapeDtypeStruct((ms, d), F32)),
        grid=(m // tm, f // tf),
        in_specs=[
            pl.BlockSpec((tm, d), lambda i, j: (i, 0), **once),
            pl.BlockSpec((ms, d), lambda i, j: (0, 0), **once),
            pl.BlockSpec((None, None, d, tf), lambda i, j: (layer, which, 0, j)),
            pl.BlockSpec((None, None, d, tf), lambda i, j: (layer, which, 0, j)),
            pl.BlockSpec((None, None, tf, d), lambda i, j: (layer, which, j, 0)),
        ],
        out_specs=(pl.BlockSpec((tm, d), lambda i, j: (i, 0), **once),
                   pl.BlockSpec((ms, d), lambda i, j: (0, 0), **once)),
        compiler_params=_cparams(("arbitrary", "arbitrary")),
        name="ffn",
    )(x, xs, wg, wu, wd)


def _mm_kernel(x_ref, w_ref, o_ref, *, kc):
    k = x_ref.shape[1]
    acc = None
    for k0 in range(0, k, kc):
        p = jnp.dot(x_ref[:, k0:k0 + kc].astype(BF16), w_ref[k0:k0 + kc, :].astype(BF16),
                    preferred_element_type=F32)
        acc = p if acc is None else acc + p
    o_ref[...] = acc.astype(o_ref.dtype)


def mm(x, w, *, n, tn, col_block=0, tm=1024, out_dtype=F32, lead=None):
    m, k = x.shape
    tm = min(tm, m)
    lead = tuple(lead or ())
    assert w.ndim == 2 + len(lead) and w.shape[-2] == k and n % tn == 0 and m % tm == 0
    assert (col_block * tn + n) <= w.shape[-1]
    wspec = pl.BlockSpec((None,) * len(lead) + (k, tn), lambda i, j: lead + (0, j + col_block))
    return pl.pallas_call(
        functools.partial(_mm_kernel, kc=min(1024, k)),
        out_shape=jax.ShapeDtypeStruct((m, n), out_dtype),
        grid=(m // tm, n // tn),
        in_specs=[pl.BlockSpec((tm, k), lambda i, j: (i, 0)), wspec],
        out_specs=pl.BlockSpec((tm, tn), lambda i, j: (i, j)),
        compiler_params=_cparams(("parallel", "arbitrary")),
        name="mm",
    )(x, w)


def _iota(shape, dim):
    return lax.broadcasted_iota(jnp.int32, shape, dim)


def _conv_silu(x, w_ref, b_ref=None):
    row = _iota(x.shape, 0)
    acc = x * w_ref[CONV_W - 1:CONV_W, :]
    for k in range(1, CONV_W):
        shifted = jnp.where(row >= k, pltpu.roll(x, k, 0), 0.0)
        acc = acc + shifted * w_ref[CONV_W - 1 - k:CONV_W - k, :]
    if b_ref is not None:
        acc = acc + b_ref[...]
    return _silu(acc)


def _chunk_cumsum(x, chunk):
    pos = _iota(x.shape, 0) & (chunk - 1)
    k = 1
    while k < chunk:
        x = x + jnp.where(pos >= k, pltpu.roll(x, k, 0), 0.0)
        k *= 2
    return x


def _lane_select_matrix(sel):
    return (_iota((LANES, LANES), 0) == sel).astype(BF16)


def _split2(a):
    hi = a.astype(BF16)
    return hi, (a - hi.astype(F32)).astype(BF16)


def _expand(x, pick):
    h1 = x.astype(BF16)
    r1 = x - h1.astype(F32)
    h2 = r1.astype(BF16)
    h3 = (r1 - h2.astype(F32)).astype(BF16)
    dot = functools.partial(jnp.dot, preferred_element_type=F32)
    return dot(h1, pick) + dot(h2, pick) + dot(h3, pick)


def _dot3(a_parts, b_parts):
    (ah, al), (bh, bl) = a_parts, b_parts
    dot = functools.partial(jnp.dot, preferred_element_type=F32)
    return dot(ah, bh) + (dot(al, bh) + dot(ah, bl))


def _ssd_kernel(xs_ref, bm_ref, cm_ref, wx_ref, wb_ref, wc_ref, bx_ref, bb_ref, bc_ref,
                dt_ref, dtb_ref, alog_ref, dsk_ref, z_ref, y_ref, st_ref,
                xs_s, bm_s, cm_s, cum_s, dte_s, dt_s, la_s, *, pairs_per_group):
    hp = pl.program_id(1)
    t_len = xs_ref.shape[1]
    half = LANES // 2
    xs_s[...] = _conv_silu(xs_ref[0], wx_ref, bx_ref)

    @pl.when(hp % pairs_per_group == 0)
    def _():
        bm_s[...] = _conv_silu(bm_ref[0], wb_ref, bb_ref)
        cm_s[...] = _conv_silu(cm_ref[0], wc_ref, bc_ref)

    @pl.when(hp == 0)
    def _():
        dt = _softplus(dt_ref[0] + dtb_ref[...])
        dt_s[...] = dt
        la_s[...] = dt * (-jnp.exp(alog_ref[...]))

    lane_row = _iota((1, LANES), 1)
    pick = _lane_select_matrix(2 * hp + (lane_row >= half).astype(jnp.int32))
    dte_s[...] = _expand(dt_s[...], pick)
    cum_s[...] = _chunk_cumsum(_expand(la_s[...], pick), CHUNK)

    t_io = _iota((CHUNK, LANES), 0)
    s_io = _iota((CHUNK, LANES), 1) & (half - 1)
    causal = s_io <= t_io
    same_head = (_iota((LANES, LANES), 0) >= half) == (_iota((LANES, LANES), 1) >= half)
    low = lane_row < half
    zeros = jnp.zeros((CHUNK, LANES), F32)
    dskip = dsk_ref[...]

    blk = 2 * CHUNK
    n_blocks = t_len // blk
    group = 4 if n_blocks % 4 == 0 else 1

    def each(f, *lists):
        return [f(*args) for args in zip(*lists)]

    def row_form(m_t, cc):
        a0, a1 = m_t[0:1, :], m_t[half:half + 1, :]
        if cc == 0:
            return jnp.where(low, a0, pltpu.roll(a1, half, 1))
        return jnp.where(low, pltpu.roll(a0, half, 1), a1)

    def blocks(i, st):
        rows = [pl.ds(pl.multiple_of((group * i + b_i) * blk, blk), blk) for b_i in range(group)]
        x2, b2, c2, cu2, dt2 = ([ref[r, :] for r in rows] for ref in (xs_s, bm_s, cm_s, cum_s, dte_s))
        cu_t, dt_t, b_t = (each(lambda a: a.T, l) for l in (cu2, dt2, b2))
        ccs = [cc for _ in range(group) for cc in range(2)]
        sl = [slice(cc * CHUNK, (cc + 1) * CHUNK) for cc in ccs]
        per_chunk = lambda l: [l[n // 2][sl[n]] for n in range(2 * group)]
        xc, bc, cmc, ce, de = (per_chunk(l) for l in (x2, b2, c2, cu2, dt2))
        cum_row = [row_form(cu_t[n // 2], ccs[n]) for n in range(2 * group)]
        dt_row = [row_form(dt_t[n // 2], ccs[n]) for n in range(2 * group)]
        cb = each(lambda c, b: _bdot_nt(c, jnp.concatenate([b, b], axis=0)), cmc, bc)
        wp = each(lambda cb_, ce_, cr, dr: cb_ * jnp.exp(jnp.where(causal, ce_ - cr, -jnp.inf)) * dr,
                  cb, ce, cum_row, dt_row)
        bd = each(lambda x: jnp.where(same_head, jnp.concatenate([x, x], axis=0), 0.0), xc)
        y_intra = each(_bdot, wp, bd)
        cl = each(lambda ce_: ce_[CHUNK - 1:CHUNK, :], ce)
        tx = each(lambda cl_, ce_, de_, x: jnp.exp(cl_ - ce_) * de_ * x, cl, ce, de, xc)
        tx_pad = [jnp.concatenate([t_, zeros] if cc == 0 else [zeros, t_], axis=0)
                  for t_, cc in zip(tx, ccs)]
        upd = [_bdot(b_t[n // 2], tx_pad[n]) for n in range(2 * group)]
        dec = each(jnp.exp, cl)
        starts = []
        for n in range(2 * group):
            starts.append(st)
            st = st * dec[n] + upd[n]
        y_inter = each(_bdot, cmc, starts)
        y = each(lambda yi, ce_, yn, x: yi + jnp.exp(ce_) * yn + x * dskip, y_intra, ce, y_inter, xc)
        for b_i, r in enumerate(rows):
            y_ref[0, r, :] = jnp.concatenate(y[2 * b_i:2 * b_i + 2], axis=0) * _silu(z_ref[0, r, :])
        return st

    st = lax.fori_loop(0, n_blocks // group, blocks, jnp.zeros((LANES, LANES), F32))
    st_ref[0] = st.T


def ssd_prompt(xbc, dt_raw, z, conv_w, conv_b, dt_bias, a_log, d_skip):
    b, t, conv_dim = xbc.shape
    d_inner = z.shape[-1]
    heads = d_inner // MB_HEADDIM
    n = (conv_dim - d_inner) // (2 * MB_GROUPS)
    assert heads == LANES and n == LANES and t % (2 * CHUNK) == 0
    pairs = heads // 2
    ppg = pairs // MB_GROUPS
    xoff, boff = 0, d_inner // LANES
    coff = boff + MB_GROUPS

    def seq(off, grouped):
        if grouped:
            return lambda bi, hp: (bi, 0, off + hp // ppg)
        return lambda bi, hp: (bi, 0, off + hp)

    def par(off, grouped):
        if grouped:
            return lambda bi, hp: (0, off + hp // ppg)
        return lambda bi, hp: (0, off + hp)

    slab = lambda f: pl.BlockSpec((1, t, LANES), f)
    wsp = lambda f: pl.BlockSpec((CONV_W, LANES), f)
    vsp = lambda f: pl.BlockSpec((1, LANES), f)
    cb2 = conv_b.reshape(1, conv_dim)
    dsk = jnp.repeat(d_skip, MB_HEADDIM).reshape(1, d_inner)
    return pl.pallas_call(
        functools.partial(_ssd_kernel, pairs_per_group=ppg),
        out_shape=(jax.ShapeDtypeStruct((b, t, d_inner), F32),
                   jax.ShapeDtypeStruct((b, d_inner, n), F32)),
        grid=(b, pairs),
        in_specs=[slab(seq(xoff, False)), slab(seq(boff, True)), slab(seq(coff, True)),
                  wsp(par(xoff, False)), wsp(par(boff, True)), wsp(par(coff, True)),
                  vsp(par(xoff, False)), vsp(par(boff, True)), vsp(par(coff, True)),
                  slab(lambda bi, hp: (bi, 0, 0)), vsp(lambda bi, hp: (0, 0)), vsp(lambda bi, hp: (0, 0)),
                  vsp(lambda bi, hp: (0, hp)), slab(lambda bi, hp: (bi, 0, hp))],
        out_specs=(slab(lambda bi, hp: (bi, 0, hp)),
                   pl.BlockSpec((1, LANES, LANES), lambda bi, hp: (bi, hp, 0))),
        scratch_shapes=[pltpu.VMEM((t, LANES), F32)] * 7,
        compiler_params=_cparams(("arbitrary", "arbitrary")),
        name="ssd_prompt",
    )(xbc, xbc, xbc, conv_w, conv_w, conv_w, cb2, cb2, cb2,
      dt_raw, dt_bias.reshape(1, heads), a_log.reshape(1, heads), dsk, z)


def mamba_prompt(u, b, t, j, w_in, conv_w, conv_b, dt_bias, a_log, d_skip, norm_g, w_out):
    m, d = u.shape
    d_inner = norm_g.shape[-1]
    heads = dt_bias.shape[-1]
    conv_dim = conv_w.shape[-1]
    z = mm(u, w_in, n=d_inner, tn=512, lead=(j,))
    xbc = mm(u, w_in, n=conv_dim, tn=512, col_block=d_inner // 512, lead=(j,))
    dt_raw = mm(u, w_in, n=heads, tn=LANES, col_block=(d_inner + conv_dim) // LANES, lead=(j,))
    xbc3 = xbc.reshape(b, t, conv_dim)
    gated, state = ssd_prompt(xbc3, dt_raw.reshape(b, t, heads), z.reshape(b, t, d_inner),
                              conv_w[j], conv_b[j], dt_bias[j], a_log[j], d_skip[j])
    yn = prenorm(gated.reshape(m, d_inner), norm_g[j], tm=128)
    mix = mm(yn, w_out, n=d, tn=256, tm=1024, lead=(j,))
    n = state.shape[-1]
    return mix, state.reshape(b, heads, MB_HEADDIM, n), xbc3[:, t - (CONV_W - 1):, :]


def _l2norm(x):
    return x * lax.rsqrt(jnp.sum(x * x, axis=-1, keepdims=True) + NORM_EPS)


def _gdn_kernel(q_ref, k_ref, v_ref, gate_ref, wq_ref, wk_ref, wv_ref, ba_ref, dtb_ref, alog_ref,
                g_ref, o_ref, st_ref, q_s, k_s, v_s, beta_s, cum_s, u_s, w_s, a_s, *, heads, hps):
    hg = pl.program_id(1)
    t_len = q_ref.shape[1]
    dk = q_ref.shape[2] // hps
    ba = ba_ref[0]
    beta_all = jax.nn.sigmoid(ba)
    lg_all = -jnp.exp(alog_ref[...]) * _softplus(ba + dtb_ref[...])
    for hh in range(hps):
        cols = slice(hh * dk, (hh + 1) * dk)
        h = hg * hps + hh
        q_s[hh] = _l2norm(_conv_silu(q_ref[0, :, cols], wq_ref[:, cols])) * (dk ** -0.5)
        k_s[hh] = _l2norm(_conv_silu(k_ref[0, :, cols], wk_ref[:, cols]))
        v_s[hh] = _conv_silu(v_ref[0, :, cols], wv_ref[:, cols])
        beta_s[hh] = _expand(beta_all, _lane_select_matrix(h))
        cum_s[hh] = _chunk_cumsum(_expand(lg_all, _lane_select_matrix(heads + h)), CHUNK)

    blk = 2 * CHUNK
    r_io, c_io = _iota((blk, blk), 0), _iota((blk, blk), 1)
    same_chunk = (r_io >= CHUNK) == (c_io >= CHUNK)
    strict = same_chunk & (c_io < r_io)
    incl = same_chunk & (c_io <= r_io)
    eye = (r_io == c_io).astype(F32)
    lane_row = _iota((1, blk), 1)
    zeros = jnp.zeros((CHUNK, dk), F32)

    n_blocks = t_len // blk
    group = 4 if n_blocks % 4 == 0 else 1

    def each(f, *lists):
        return [f(*args) for args in zip(*lists)]

    def solve_blocks(hh, i, carry):
        rows = [pl.ds(pl.multiple_of((group * i + b_i) * blk, blk), blk) for b_i in range(group)]
        q2, k2, v2 = ([ref[hh, r, :] for r in rows] for ref in (q_s, k_s, v_s))
        be2, g2 = ([ref[hh, r, :] for r in rows] for ref in (beta_s, cum_s))
        diff = each(lambda g: g - g.T[0:1, :], g2)
        n1 = each(lambda be, k, d: -(be * _bdot_nt(k, k) * jnp.exp(jnp.where(strict, d, -jnp.inf))),
                  be2, k2, diff)
        square = lambda s: _dot3(s, s)
        pair = lambda a, b, sa, sb: eye + a + b + _dot3(sa, sb)
        s1 = each(_split2, n1)
        n2 = each(square, s1)
        s2 = each(_split2, n2)
        n4 = each(square, s2)
        f12 = each(pair, n1, n2, s1, s2)
        s4 = each(_split2, n4)
        n8 = each(square, s4)
        s8 = each(_split2, n8)
        n16 = each(square, s8)
        f48 = each(pair, n4, n8, s4, s8)
        s16 = each(_split2, n16)
        n32 = each(square, s16)
        f1632 = each(pair, n16, n32, s16, each(_split2, n32))
        f1248 = each(_dot3, each(_split2, f12), each(_split2, f48))
        t_inv = each(_dot3, each(_split2, f1248), each(_split2, f1632))
        rhs = each(lambda v, k, be, g: jnp.concatenate([v * be, k * (be * jnp.exp(g))], axis=1),
                   v2, k2, be2, g2)
        uw = each(_dot3, each(_split2, t_inv), each(_split2, rhs))
        attn = each(lambda q, k, d: _bdot_nt(q, k) * jnp.exp(jnp.where(incl, d, -jnp.inf)),
                    q2, k2, diff)
        for r, uw_b, a_b in zip(rows, uw, attn):
            u_s[hh, r, :] = uw_b[:, :dk]
            w_s[hh, r, :] = uw_b[:, dk:]
            a_s[hh, r, :] = a_b
        return carry

    for hh in range(hps):
        lax.fori_loop(0, n_blocks // group, functools.partial(solve_blocks, hh), 0)

    def pair_of_chunks(i, sts):
        rows = pl.ds(pl.multiple_of(i * blk, blk), blk)
        hs = list(range(hps))
        sts = list(sts)
        q2, k2, g2 = ([ref[hh, rows, :] for hh in hs] for ref in (q_s, k_s, cum_s))
        u2, w2 = ([ref[hh, rows, :] for hh in hs] for ref in (u_s, w_s))
        g_row = each(lambda g: g.T[0:1, :], g2)
        k_t = each(lambda k: k.T, k2)
        qe2 = each(lambda q, g: q * jnp.exp(g), q2, g2)
        vns = [[] for _ in hs]
        o_inter = [[] for _ in hs]
        for cc in range(2):
            sl = slice(cc * CHUNK, (cc + 1) * CHUNK)
            in_chunk = (lane_row >= cc * CHUNK) & (lane_row < (cc + 1) * CHUNK)
            vn = [u[sl] - _bdot(w[sl], st) for u, w, st in zip(u2, w2, sts)]
            for hh in hs:
                o_inter[hh].append(_bdot(qe2[hh][sl], sts[hh]))
            g_last = [g[(cc + 1) * CHUNK - 1:(cc + 1) * CHUNK, :] for g in g2]
            kd = [kt * jnp.where(in_chunk, jnp.exp(gl - gr), 0.0)
                  for kt, gl, gr in zip(k_t, g_last, g_row)]
            vn_pad = [jnp.concatenate([v, zeros] if cc == 0 else [zeros, v], axis=0) for v in vn]
            sts = [st * jnp.exp(gl) + _bdot(kd_, vp) for st, gl, kd_, vp in zip(sts, g_last, kd, vn_pad)]
            for hh in hs:
                vns[hh].append(vn[hh])
        for hh in hs:
            cols = slice(hh * dk, (hh + 1) * dk)
            o2 = (jnp.concatenate(o_inter[hh], axis=0)
                  + _bdot(a_s[hh, rows, :], jnp.concatenate(vns[hh], axis=0)))
            o_ref[0, rows, cols] = (_rms(o2, g_ref[...])
                                    * _silu(gate_ref[0, rows, cols])).astype(o_ref.dtype)
        return tuple(sts)

    sts = lax.fori_loop(0, t_len // blk, pair_of_chunks,
                        tuple(jnp.zeros((dk, dk), F32) for _ in range(hps)))
    for hh in range(hps):
        st_ref[0, hh] = sts[hh]


GDN_HEADS_PER_STEP = 2


def gdn_prompt(qkvg, ba, conv_w, dt_bias, a_log, norm_g):
    b, t, w4 = qkvg.shape
    width = w4 // 4
    dk = norm_g.shape[-1]
    heads = width // dk
    hps = GDN_HEADS_PER_STEP
    assert dk == LANES and 2 * heads <= LANES and t % (2 * CHUNK) == 0 and heads % hps == 0
    pad = LANES - 2 * heads
    dtb = jnp.pad(dt_bias, (heads, pad)).reshape(1, LANES)
    alog = jnp.pad(a_log, (heads, pad)).reshape(1, LANES)
    hgs = heads // hps
    slab = lambda seg: pl.BlockSpec((1, t, hps * dk), lambda bi, h: (bi, 0, seg * hgs + h))
    wsp = lambda seg: pl.BlockSpec((CONV_W, hps * dk), lambda bi, h: (0, seg * hgs + h))
    vec = pl.BlockSpec((1, LANES), lambda bi, h: (0, 0))
    return pl.pallas_call(
        functools.partial(_gdn_kernel, heads=heads, hps=hps),
        out_shape=(jax.ShapeDtypeStruct((b, t, width), BF16),
                   jax.ShapeDtypeStruct((b, heads, dk, dk), F32)),
        grid=(b, hgs),
        in_specs=[slab(0), slab(1), slab(2), slab(3), wsp(0), wsp(1), wsp(2),
                  pl.BlockSpec((1, t, LANES), lambda bi, h: (bi, 0, 0)), vec, vec, vec],
        out_specs=(pl.BlockSpec((1, t, hps * dk), lambda bi, h: (bi, 0, h)),
                   pl.BlockSpec((1, hps, dk, dk), lambda bi, h: (bi, h, 0, 0))),
        scratch_shapes=[pltpu.VMEM((hps, t, dk), F32)] * 8,
        compiler_params=_cparams(("parallel", "arbitrary")),
        name="gdn_prompt",
    )(qkvg, qkvg, qkvg, qkvg, conv_w, conv_w, conv_w, ba, dtb, alog, norm_g.reshape(1, dk))


def _tail_cols(w, start, lead):
    cols = w[lead][:, start:]
    return jnp.pad(cols, ((0, 0), (0, LANES - cols.shape[1])))


def gdn_prompt_mixer(u, b, t, j, w_in, conv_w, dt_bias, a_log, norm_g, w_out):
    m, d = u.shape
    width = conv_w.shape[-1] // 3
    qkvg = mm(u, w_in, n=4 * width, tn=512, lead=(j,))
    ba = mm(u, _tail_cols(w_in, 4 * width, j), n=LANES, tn=LANES)
    o, state = gdn_prompt(qkvg.reshape(b, t, 4 * width), ba.reshape(b, t, LANES), conv_w[j],
                          dt_bias[j], a_log[j], norm_g[j])
    mix = mm(o.reshape(m, width), w_out, n=d, tn=512, lead=(j,))
    new_buf = qkvg.reshape(b, t, 4 * width)[:, t - (CONV_W - 1):, :3 * width]
    return mix, state, new_buf


def _hgrn_lower_bound(lb_ref, layer):
    logits = lb_ref[...]
    e = jnp.exp(logits - jnp.max(logits, axis=0, keepdims=True))
    p = e / jnp.sum(e, axis=0, keepdims=True)
    lb = jnp.zeros((1, logits.shape[1]), F32)
    for r in range(1, layer + 1):
        lb = lb + p[r:r + 1, :]
    return lb


def _hgrn_kernel(q_ref, f_ref, i_ref, g_ref, lb_ref, ng_ref, o_ref, st_ref,
                 q_s, k_s, cum_s, *, layer):
    t_len = q_ref.shape[1]
    dk = q_ref.shape[2]
    lb = _hgrn_lower_bound(lb_ref, layer)
    fg = lb + (1.0 - lb) * jax.nn.sigmoid(f_ref[0])
    q_s[...] = _silu(q_ref[0]) * (dk ** -0.5)
    k_s[...] = 1.0 - fg
    cum_s[...] = _chunk_cumsum(jnp.log(fg), HG_CHUNK)

    blk = 4 * HG_CHUNK
    sub = 8
    row_blk = _iota((blk, dk), 0)
    r_io, c_io = _iota((blk, blk), 0), _iota((blk, blk), 1)
    leaf_base = r_io & ~(sub - 1)

    n_blocks = t_len // blk
    group = 2 if n_blocks % 2 == 0 else 1
    per_blk = blk // HG_CHUNK

    def each(f, *lists):
        return [f(*args) for args in zip(*lists)]

    def intra_attention(r0, q4, k4, cu4):
        attn = jnp.zeros((blk, blk), F32)
        for j in range(sub):
            pick = lambda a: jnp.broadcast_to(a.reshape(blk // sub, sub, dk)[:, j:j + 1, :],
                                              (blk // sub, sub, dk)).reshape(blk, dk)
            ok = (row_blk & (sub - 1)) >= j
            col = jnp.sum(q4 * pick(k4) * jnp.exp(jnp.where(ok, cu4 - pick(cu4), -jnp.inf)),
                          axis=1, keepdims=True)
            attn = jnp.where(c_io == leaf_base + j, col, attn)
        for size in (2 * sub, 4 * sub):
            mid = jnp.concatenate(
                [jnp.broadcast_to(cum_s[pl.ds(r0 + g * size + size // 2, 1), :], (size, dk))
                 for g in range(blk // size)], axis=0)
            upper = (row_blk & (size - 1)) >= size // 2
            qf = q4 * jnp.exp(jnp.where(upper, cu4 - mid, -jnp.inf))
            kf = k4 * jnp.exp(jnp.where(upper, -jnp.inf, mid - cu4))
            same = (r_io & ~(size - 1)) == (c_io & ~(size - 1))
            attn = attn + jnp.where(same, _bdot_nt(qf, kf), 0.0)
        return attn

    def blocks(i, st_t):
        r0s = [pl.multiple_of((group * i + b_i) * blk, blk) for b_i in range(group)]
        rows = [pl.ds(r0, blk) for r0 in r0s]
        q4, k4, cu4 = ([ref[r, :] for r in rows] for ref in (q_s, k_s, cum_s))
        v4 = [i_ref[0, r, :] for r in rows]
        v4_t = each(lambda v: v.T, v4)
        o_intra = each(_bdot, each(intra_attention, r0s, q4, k4, cu4), v4)
        qe4 = each(lambda q, cu: q * jnp.exp(cu), q4, cu4)
        chunks = [(b_i, cc * HG_CHUNK) for b_i in range(group) for cc in range(per_blk)]
        last = [cu4[b_i][c0 + HG_CHUNK - 1:c0 + HG_CHUNK, :] for b_i, c0 in chunks]
        upd = [_bdot(v4_t[b_i],
                     jnp.where((row_blk >= c0) & (row_blk < c0 + HG_CHUNK),
                               k4[b_i] * jnp.exp(l - cu4[b_i]), 0.0))
               for (b_i, c0), l in zip(chunks, last)]
        dec = each(jnp.exp, last)
        starts = []
        for n in range(len(chunks)):
            starts.append(st_t)
            st_t = st_t * dec[n] + upd[n]
        o_inter = [_bdot_nt(qe4[b_i][c0:c0 + HG_CHUNK], s) for (b_i, c0), s in zip(chunks, starts)]
        for b_i, r in enumerate(rows):
            o4 = o_intra[b_i] + jnp.concatenate(o_inter[b_i * per_blk:(b_i + 1) * per_blk], axis=0)
            o_ref[0, r, :] = (_rms(o4, ng_ref[...]) * _silu(g_ref[0, r, :])).astype(o_ref.dtype)
        return st_t

    st_t = lax.fori_loop(0, n_blocks // group, blocks, jnp.zeros((dk, dk), F32))
    st_ref[0, 0] = st_t.T


def hgrn_prompt(qfig, lb_logits, norm_g, layer):
    b, t, d4 = qfig.shape
    d = d4 // 4
    dk = norm_g.shape[-1]
    heads = d // dk
    depth = lb_logits.shape[0]
    assert dk == LANES and t % (4 * HG_CHUNK) == 0
    slab = lambda off: pl.BlockSpec((1, t, dk), lambda bi, h: (bi, 0, off + h))
    return pl.pallas_call(
        functools.partial(_hgrn_kernel, layer=layer),
        out_shape=(jax.ShapeDtypeStruct((b, t, d), BF16),
                   jax.ShapeDtypeStruct((b, heads, dk, dk), F32)),
        grid=(b, heads),
        in_specs=[slab(0), slab(heads), slab(2 * heads), slab(3 * heads),
                  pl.BlockSpec((depth, dk), lambda bi, h: (0, h)),
                  pl.BlockSpec((1, dk), lambda bi, h: (0, 0))],
        out_specs=(pl.BlockSpec((1, t, dk), lambda bi, h: (bi, 0, h)),
                   pl.BlockSpec((1, 1, dk, dk), lambda bi, h: (bi, h, 0, 0))),
        scratch_shapes=[pltpu.VMEM((t, dk), F32)] * 3,
        compiler_params=_cparams(("parallel", "arbitrary")),
        name="hgrn_prompt",
    )(qfig, qfig, qfig, qfig, lb_logits, norm_g.reshape(1, dk))


def hgrn_prompt_mixer(u, b, t, j, layer, w_in, lb_logits, norm_g, w_out):
    m, d = u.shape
    qfig = mm(u, w_in, n=4 * d, tn=512, lead=(j,))
    o, state = hgrn_prompt(qfig.reshape(b, t, 4 * d), lb_logits, norm_g[j], layer)
    mix = mm(o.reshape(m, d), w_out, n=d, tn=512, lead=(j,))
    return mix, state


def rope_tables(pos, rope_dim):
    half = rope_dim // 2
    inv = ML_ROPE_THETA ** (-jnp.arange(half, dtype=F32) / half)
    ang = pos.astype(F32)[:, None] * inv
    cos, sin = jnp.cos(ang), jnp.sin(ang)
    reps = LANES // rope_dim
    return (jnp.tile(jnp.concatenate([cos, cos], axis=-1), (1, reps)),
            jnp.tile(jnp.concatenate([-sin, sin], axis=-1), (1, reps)))


def _rope_kernel(x_ref, cos_ref, sin_ref, o_ref, *, rope_dim, dup):
    half = rope_dim // 2
    lane = _iota((x_ref.shape[0], LANES), 1)
    first = (lane & (rope_dim - 1)) < half
    cos, sin = cos_ref[...], sin_ref[...]
    for c in range(x_ref.shape[1] // LANES):
        x = x_ref[:, c * LANES:(c + 1) * LANES]
        other = jnp.where(first, pltpu.roll(x, LANES - half, 1), pltpu.roll(x, half, 1))
        y = x * cos + other * sin
        if dup:
            y = y + pltpu.roll(y, rope_dim, 1)
        o_ref[:, c * LANES:(c + 1) * LANES] = y.astype(o_ref.dtype)


def rope(x, cos, sin, rope_dim, out_dtype, dup=False, tm=256):
    m, w = x.shape
    t = cos.shape[0]
    if t == 1:
        tm = min(tm, m)
        tab = pl.BlockSpec((1, LANES), lambda i: (0, 0))
    else:
        tm = min(tm, t)
        nt = t // tm
        tab = pl.BlockSpec((tm, LANES), lambda i: (i % nt, 0))
    return pl.pallas_call(
        functools.partial(_rope_kernel, rope_dim=rope_dim, dup=dup),
        out_shape=jax.ShapeDtypeStruct((m, w), out_dtype),
        grid=(m // tm,),
        in_specs=[pl.BlockSpec((tm, w), lambda i: (i, 0)), tab, tab],
        out_specs=pl.BlockSpec((tm, w), lambda i: (i, 0)),
        compiler_params=_cparams(("parallel",)),
        name="rope",
    )(x, cos, sin)


def _mla_attn_kernel(qn_ref, qp_ref, kn_ref, kp_ref, v_ref, o_ref, *, tq, scale, rope_dim):
    h = pl.program_id(1)
    t_len = qn_ref.shape[1]
    lane = _iota((tq, LANES), 1)
    mine = (lane // rope_dim) == (h % (LANES // rope_dim))
    for q0 in range(0, t_len, tq):
        kv_len = q0 + tq
        qn = qn_ref[0, q0:q0 + tq, :]
        qp = jnp.where(mine, qp_ref[0, q0:q0 + tq, :], 0.0)
        s = (_bdot_nt(qn, kn_ref[0, :kv_len, :]) + _bdot_nt(qp, kp_ref[0, :kv_len, :])) * scale
        ok = _iota((tq, kv_len), 1) <= _iota((tq, kv_len), 0) + q0
        s = jnp.where(ok, s, -jnp.inf)
        p = jnp.exp(s - jnp.max(s, axis=-1, keepdims=True))
        l = jnp.sum(p, axis=-1, keepdims=True)
        o_ref[0, q0:q0 + tq, :] = (_bdot(p, v_ref[0, :kv_len, :]) / l).astype(o_ref.dtype)


def mla_prompt_attention(qn, qp, kv, kp, rope_dim):
    b, t, w = qn.shape
    heads = w // ML_NOPE
    tq = min(512, t)
    scale = (ML_NOPE + rope_dim) ** -0.5
    per_tile = LANES // rope_dim
    slab = lambda f: pl.BlockSpec((1, t, LANES), f)
    return pl.pallas_call(
        functools.partial(_mla_attn_kernel, tq=tq, scale=scale, rope_dim=rope_dim),
        out_shape=jax.ShapeDtypeStruct((b, t, heads * ML_V), BF16),
        grid=(b, heads),
        in_specs=[slab(lambda bi, h: (bi, 0, h)), slab(lambda bi, h: (bi, 0, h // per_tile)),
                  slab(lambda bi, h: (bi, 0, 2 * h)), slab(lambda bi, h: (bi, 0, 0)),
                  slab(lambda bi, h: (bi, 0, 2 * h + 1))],
        out_specs=slab(lambda bi, h: (bi, 0, h)),
        compiler_params=_cparams(("parallel", "arbitrary")),
        name="mla_attn",
    )(qn, qp, kv, kp, kv)


def mla_project(u, pos, j, w_a, q_norm, kv_norm, w_qb):
    q_rank, kv_rank = q_norm.shape[-1], kv_norm.shape[-1]
    rope_dim = w_a.shape[-1] - q_rank - kv_rank
    a1 = mm(u, w_a, n=q_rank + kv_rank, tn=512, lead=(j,))
    kpe_raw = mm(u, _tail_cols(w_a, q_rank + kv_rank, j), n=LANES, tn=LANES)
    q_c = prenorm(a1, q_norm[j], width=q_rank)
    ckv = prenorm(a1, kv_norm[j], out_dtype=F32, col_block=q_rank // kv_rank, width=kv_rank)
    cos, sin = rope_tables(pos, rope_dim)
    k_pe = rope(kpe_raw, cos, sin, rope_dim, F32, dup=True)
    wq = w_qb[j].reshape(q_rank, ML_HEADS, ML_NOPE + rope_dim)
    w_nope = wq[:, :, :ML_NOPE].reshape(q_rank, ML_HEADS * ML_NOPE)
    w_pe = wq[:, :, ML_NOPE:].reshape(q_rank, ML_HEADS * rope_dim)
    q_nope = mm(q_c, w_nope, n=ML_HEADS * ML_NOPE, tn=512, out_dtype=BF16)
    q_pe = rope(mm(q_c, w_pe, n=ML_HEADS * rope_dim, tn=512), cos, sin, rope_dim, BF16)
    return q_nope, q_pe, ckv, k_pe, rope_dim


def mla_prompt_mixer(u, b, t, j, w_a, q_norm, kv_norm, w_qb, w_kvb, w_o):
    m, d = u.shape
    q_nope, q_pe, ckv, k_pe, rope_dim = mla_project(u, jnp.arange(t), j, w_a, q_norm, kv_norm, w_qb)
    kv = mm(ckv, w_kvb, n=w_kvb.shape[-1], tn=512, out_dtype=BF16, lead=(j,))
    o = mla_prompt_attention(q_nope.reshape(b, t, -1), q_pe.reshape(b, t, -1), kv.reshape(b, t, -1),
                             k_pe.astype(BF16).reshape(b, t, LANES), rope_dim)
    mix = mm(o.reshape(m, -1), w_o, n=d, tn=512, lead=(j,))
    rows = jnp.concatenate([ckv, k_pe[:, :rope_dim]], axis=-1).reshape(b, t, -1)
    return mix, rows


def _lane_pick(x, onehot):
    return jnp.sum(jnp.where(onehot, x, 0.0), axis=1, keepdims=True)


_HEADS_PER_ITER_SSD = 16
_HEADS_PER_ITER = 8


def conv_step(x, buf, w, b=None):
    xp = jnp.concatenate([buf, x[:, None, :]], axis=1)
    y = xp[:, 0] * w[0]
    for k in range(1, CONV_W):
        y = y + xp[:, k] * w[k]
    if b is not None:
        y = y + b
    return y, xp[:, 1:]


def _ssd_step_kernel(xs_ref, dt_ref, dtb_ref, alog_ref, dsk_ref, bm_ref, cm_ref, s_ref,
                     y_ref, so_ref, *, heads_per_group):
    heads = s_ref.shape[1]
    xs_t = xs_ref[0]
    dt = _softplus(dt_ref[0] + dtb_ref[...])
    dec = jnp.exp(dt * (-jnp.exp(alog_ref[...])))
    dtxs = xs_t * dt
    lane = _iota((1, heads), 1)
    cb = jnp.zeros((1, heads), F32)
    for g in range(bm_ref.shape[1]):
        cb_g = jnp.sum(cm_ref[0, g:g + 1, :] * bm_ref[0, g:g + 1, :], axis=1, keepdims=True)
        cb = jnp.where(lane // heads_per_group == g, cb_g, cb)

    hb = _HEADS_PER_ITER_SSD
    assert heads_per_group % hb == 0

    def head_block(i, y_acc):
        hs = [i * hb + k for k in range(hb)]
        g = (i * hb) // heads_per_group
        cm_row, bm_row = cm_ref[0, pl.ds(g, 1), :], bm_ref[0, pl.ds(g, 1), :]
        s = [s_ref[0, h] for h in hs]
        onehot = [lane == h for h in hs]
        dec_h = [_lane_pick(dec, o) for o in onehot]
        dx_col = [_lane_pick(dtxs, o) for o in onehot]
        y_col = [jnp.sum(s_ * cm_row, axis=1, keepdims=True) * d for s_, d in zip(s, dec_h)]
        for h, s_, d, x in zip(hs, s, dec_h, dx_col):
            so_ref[0, h] = s_ * d + x * bm_row
        for o, y in zip(onehot, y_col):
            y_acc = jnp.where(o, y, y_acc)
        return y_acc

    y_acc = lax.fori_loop(0, heads // hb, head_block, jnp.zeros(xs_t.shape, F32))
    y_ref[0] = y_acc + cb * dtxs + xs_t * dsk_ref[...]


def ssd_step(xs, dt_raw, bm, cm, state, dt_bias, a_log, d_skip):
    b, heads, p, n = state.shape
    groups = bm.shape[1]
    xs_t = jnp.swapaxes(xs.reshape(b, heads, p), 1, 2)
    row = lambda a: a.reshape(1, heads)
    vec = pl.BlockSpec((1, heads), lambda i: (0, 0))
    y_t, new_state = pl.pallas_call(
        functools.partial(_ssd_step_kernel, heads_per_group=heads // groups),
        out_shape=(jax.ShapeDtypeStruct((b, p, heads), F32), jax.ShapeDtypeStruct(state.shape, F32)),
        grid=(b,),
        in_specs=[pl.BlockSpec((1, p, heads), lambda i: (i, 0, 0)),
                  pl.BlockSpec((1, 1, heads), lambda i: (i, 0, 0)), vec, vec, vec,
                  pl.BlockSpec((1, groups, n), lambda i: (i, 0, 0)),
                  pl.BlockSpec((1, groups, n), lambda i: (i, 0, 0)),
                  pl.BlockSpec((1, heads, p, n), lambda i: (i, 0, 0, 0))],
        out_specs=(pl.BlockSpec((1, p, heads), lambda i: (i, 0, 0)),
                   pl.BlockSpec((1, heads, p, n), lambda i: (i, 0, 0, 0))),
        compiler_params=_cparams(("parallel",)),
        name="ssd_step",
    )(xs_t, dt_raw.reshape(b, 1, heads), row(dt_bias), row(a_log), row(d_skip), bm, cm, state)
    return jnp.swapaxes(y_t, 1, 2).reshape(b, heads * p), new_state


def _gdn_step_kernel(q_ref, k_ref, v_ref, gate_ref, beta_ref, eg_ref, ng_ref, s_ref, o_ref, so_ref):
    heads = s_ref.shape[1]
    q_t, k_t = q_ref[0], k_ref[0]
    lane = _iota((1, heads), 1)

    hb = _HEADS_PER_ITER
    col_sum = lambda x: jnp.sum(x, axis=0, keepdims=True)

    def head_block(i, carry):
        hs = [i * hb + k for k in range(hb)]
        s = [s_ref[0, h] for h in hs]
        onehot = [lane == h for h in hs]
        q_col = [_lane_pick(q_t, o) for o in onehot]
        k_col = [_lane_pick(k_t, o) for o in onehot]
        beta = [_lane_pick(beta_ref[0], o) for o in onehot]
        eg = [_lane_pick(eg_ref[0], o) for o in onehot]
        v_row = [v_ref[0, pl.ds(h, 1), :] for h in hs]
        vn = [v * b - col_sum((k * (b * e)) * s_) for v, b, e, k, s_ in zip(v_row, beta, eg, k_col, s)]
        qk = [col_sum(q * k) for q, k in zip(q_col, k_col)]
        o = [e * col_sum(q * s_) + a * n for e, q, s_, a, n in zip(eg, q_col, s, qk, vn)]
        for h, s_, e, k, n, o_h in zip(hs, s, eg, k_col, vn, o):
            so_ref[0, h] = s_ * e + k * n
            o_ref[0, pl.ds(h, 1), :] = _rms(o_h, ng_ref[...]) * _silu(gate_ref[0, pl.ds(h, 1), :])
        return carry

    lax.fori_loop(0, heads // hb, head_block, 0)


def gdn_step(q, k, v, gate, beta, eg, norm_g, state):
    b, heads, dk, dv = state.shape
    tr = lambda a: jnp.swapaxes(a, 1, 2)
    col = pl.BlockSpec((1, dk, heads), lambda i: (i, 0, 0))
    hrow = pl.BlockSpec((1, heads, dv), lambda i: (i, 0, 0))
    sc = pl.BlockSpec((1, 1, heads), lambda i: (i, 0, 0))
    st = pl.BlockSpec((1, heads, dk, dv), lambda i: (i, 0, 0, 0))
    return pl.pallas_call(
        _gdn_step_kernel,
        out_shape=(jax.ShapeDtypeStruct((b, heads, dv), F32), jax.ShapeDtypeStruct(state.shape, F32)),
        grid=(b,),
        in_specs=[col, col, hrow, hrow, sc, sc, pl.BlockSpec((1, dv), lambda i: (0, 0)), st],
        out_specs=(hrow, st),
        compiler_params=_cparams(("parallel",)),
        name="gdn_step",
    )(tr(q), tr(k), v, gate, beta.reshape(b, 1, heads), eg.reshape(b, 1, heads),
      norm_g.reshape(1, dv), state)


def _hgrn_step_kernel(q_ref, k_ref, e_ref, v_ref, gate_ref, ng_ref, s_ref, o_ref, so_ref):
    heads = s_ref.shape[1]
    q_t, k_t, e_t = q_ref[0], k_ref[0], e_ref[0]
    lane = _iota((1, heads), 1)

    hb = _HEADS_PER_ITER
    col_sum = lambda x: jnp.sum(x, axis=0, keepdims=True)

    def head_block(i, carry):
        hs = [i * hb + k for k in range(hb)]
        s = [s_ref[0, h] for h in hs]
        onehot = [lane == h for h in hs]
        q_col = [_lane_pick(q_t, o) for o in onehot]
        k_col = [_lane_pick(k_t, o) for o in onehot]
        e_col = [_lane_pick(e_t, o) for o in onehot]
        v_row = [v_ref[0, pl.ds(h, 1), :] for h in hs]
        qk = [col_sum(q * k) for q, k in zip(q_col, k_col)]
        o = [a * v + col_sum((q * e) * s_) for a, v, q, e, s_ in zip(qk, v_row, q_col, e_col, s)]
        for h, s_, e, k, v, o_h in zip(hs, s, e_col, k_col, v_row, o):
            so_ref[0, h] = s_ * e + k * v
            o_ref[0, pl.ds(h, 1), :] = _rms(o_h, ng_ref[...]) * _silu(gate_ref[0, pl.ds(h, 1), :])
        return carry

    lax.fori_loop(0, heads // hb, head_block, 0)


def hgrn_step(q, k, e, v, gate, norm_g, state):
    b, heads, dk, dv = state.shape
    tr = lambda a: jnp.swapaxes(a, 1, 2)
    col = pl.BlockSpec((1, dk, heads), lambda i: (i, 0, 0))
    hrow = pl.BlockSpec((1, heads, dv), lambda i: (i, 0, 0))
    st = pl.BlockSpec((1, heads, dk, dv), lambda i: (i, 0, 0, 0))
    return pl.pallas_call(
        _hgrn_step_kernel,
        out_shape=(jax.ShapeDtypeStruct((b, heads, dv), F32), jax.ShapeDtypeStruct(state.shape, F32)),
        grid=(b,),
        in_specs=[col, col, col, hrow, hrow, pl.BlockSpec((1, dv), lambda i: (0, 0)), st],
        out_specs=(hrow, st),
        compiler_params=_cparams(("parallel",)),
        name="hgrn_step",
    )(tr(q), tr(k), tr(e), v, gate, norm_g.reshape(1, dv), state)


def _head_in_kernel(x_ref, w_ref, o_ref):
    o_ref[0] = _bdot_nt(x_ref[...], w_ref[...])


def _head_out_kernel(x_ref, w_ref, o_ref):
    o_ref[...] = _bdot(x_ref[0], w_ref[...]).astype(o_ref.dtype)


def latent_in(q_nope, w_kvb, j):
    b = q_nope.shape[0]
    r = w_kvb.shape[-2]
    return pl.pallas_call(
        _head_in_kernel,
        out_shape=jax.ShapeDtypeStruct((ML_HEADS, b, r), F32),
        grid=(ML_HEADS,),
        in_specs=[pl.BlockSpec((b, ML_NOPE), lambda h: (0, h)),
                  pl.BlockSpec((None, r, ML_NOPE), lambda h: (j, 0, 2 * h))],
        out_specs=pl.BlockSpec((1, b, r), lambda h: (h, 0, 0)),
        compiler_params=_cparams(("parallel",)),
        name="mla_latent_in",
    )(q_nope, w_kvb)


def latent_out(o_lat, w_kvb, j):
    _, b, r = o_lat.shape
    return pl.pallas_call(
        _head_out_kernel,
        out_shape=jax.ShapeDtypeStruct((b, ML_HEADS * ML_V), BF16),
        grid=(ML_HEADS,),
        in_specs=[pl.BlockSpec((1, b, r), lambda h: (h, 0, 0)),
                  pl.BlockSpec((None, r, ML_V), lambda h: (j, 0, 2 * h + 1))],
        out_specs=pl.BlockSpec((b, ML_V), lambda h: (0, h)),
        compiler_params=_cparams(("parallel",)),
        name="mla_latent_out",
    )(o_lat, w_kvb)


PAGES_PER_STEP = 64


def _decode_kernel(pt_ref, q_ref, new_ref, *rest, scale, kv_rank):
    cache_refs = rest[:PAGES_PER_STEP]
    o_ref, m_s, l_s, acc_s = rest[PAGES_PER_STEP:]
    pg = pl.program_id(1)
    q = q_ref[0]

    @pl.when(pg == 0)
    def _():
        new = new_ref[0]
        m_s[...] = jnp.sum(q.astype(F32) * new.astype(BF16).astype(F32), axis=1, keepdims=True) * scale
        l_s[...] = jnp.ones_like(l_s)
        acc_s[...] = jnp.broadcast_to(new[:, :kv_rank], acc_s.shape)

    rows = [c[...].astype(BF16) for c in cache_refs]
    scores = [_bdot(q, r) * scale for r in rows]
    m_old = m_s[...]
    m_new = m_old
    for s in scores:
        m_new = jnp.maximum(m_new, jnp.max(s, axis=1, keepdims=True))
    alpha = jnp.exp(m_old - m_new)
    l = l_s[...] * alpha
    acc = acc_s[...] * alpha
    for s, r in zip(scores, rows):
        p = jnp.exp(s - m_new)
        l = l + jnp.sum(p, axis=1, keepdims=True)
        acc = acc + _bdot_nt(p, r[:kv_rank, :])
    m_s[...] = m_new
    l_s[...] = l
    acc_s[...] = acc

    @pl.when(pg == pl.num_programs(1) - 1)
    def _():
        o_ref[0] = acc / l


def mla_decode(q_cat, new_rows, cache, j, page_table, kv_rank):
    b, heads, row = q_cat.shape
    n_pages = page_table.shape[1]
    page = cache.shape[3]
    assert n_pages % PAGES_PER_STEP == 0
    scale = (ML_NOPE + row - kv_rank) ** -0.5

    def cache_spec(i):
        return pl.BlockSpec((None, None, row, page),
                            lambda bi, pg, pt: (j, pt[bi, pg * PAGES_PER_STEP + i], 0, 0))

    grid_spec = pltpu.PrefetchScalarGridSpec(
        num_scalar_prefetch=1,
        grid=(b, n_pages // PAGES_PER_STEP),
        in_specs=[pl.BlockSpec((1, heads, row), lambda bi, pg, pt: (bi, 0, 0)),
                  pl.BlockSpec((1, 1, row), lambda bi, pg, pt: (bi, 0, 0))]
                 + [cache_spec(i) for i in range(PAGES_PER_STEP)],
        out_specs=pl.BlockSpec((1, heads, kv_rank), lambda bi, pg, pt: (bi, 0, 0)),
        scratch_shapes=[pltpu.VMEM((heads, 1), F32), pltpu.VMEM((heads, 1), F32),
                        pltpu.VMEM((heads, kv_rank), F32)],
    )
    return pl.pallas_call(
        functools.partial(_decode_kernel, scale=scale, kv_rank=kv_rank),
        out_shape=jax.ShapeDtypeStruct((b, heads, kv_rank), F32),
        grid_spec=grid_spec,
        compiler_params=_cparams(("parallel", "arbitrary")),
        name="mla_decode",
    )(page_table, q_cat, new_rows, *([cache] * PAGES_PER_STEP))


def mamba_sample(u, j, conv_buf, state, w_in, conv_w, conv_b, dt_bias, a_log, d_skip, norm_g, w_out):
    b, d = u.shape
    d_inner = norm_g.shape[-1]
    heads = dt_bias.shape[-1]
    conv_dim = conv_w.shape[-1]
    n = state.shape[-1]
    z = mm(u, w_in, n=d_inner, tn=512, lead=(j,))
    xbc = mm(u, w_in, n=conv_dim, tn=512, col_block=d_inner // 512, lead=(j,))
    dt_raw = mm(u, w_in, n=heads, tn=LANES, col_block=(d_inner + conv_dim) // LANES, lead=(j,))
    xc, new_buf = conv_step(xbc, conv_buf, conv_w[j], conv_b[j])
    xc = _silu(xc)
    xs = xc[:, :d_inner]
    bm = xc[:, d_inner:d_inner + MB_GROUPS * n].reshape(b, MB_GROUPS, n)
    cm = xc[:, d_inner + MB_GROUPS * n:].reshape(b, MB_GROUPS, n)
    y, new_state = ssd_step(xs, dt_raw, bm, cm, state, dt_bias[j], a_log[j], d_skip[j])
    yn = prenorm(y * _silu(z), norm_g[j])
    return mm(yn, w_out, n=d, tn=256, lead=(j,)), new_state, new_buf


def gdn_sample(u, j, conv_buf, state, w_in, conv_w, dt_bias, a_log, norm_g, w_out):
    b, d = u.shape
    _, heads, dk, _ = state.shape
    width = heads * dk
    qkvg = mm(u, w_in, n=4 * width, tn=512, lead=(j,))
    ba = mm(u, _tail_cols(w_in, 4 * width, j), n=LANES, tn=LANES)
    qkv, new_buf = conv_step(qkvg[:, :3 * width], conv_buf, conv_w[j])
    qkv = _silu(qkv).reshape(b, 3, heads, dk)
    q = _l2norm(qkv[:, 0]) * (dk ** -0.5)
    k = _l2norm(qkv[:, 1])
    beta = jax.nn.sigmoid(ba[:, :heads])
    eg = jnp.exp(-jnp.exp(a_log[j]) * _softplus(ba[:, heads:2 * heads] + dt_bias[j]))
    gate = qkvg[:, 3 * width:].reshape(b, heads, dk)
    o, new_state = gdn_step(q, k, qkv[:, 2], gate, beta, eg, norm_g[j], state)
    return mm(o.reshape(b, width), w_out, n=d, tn=512, lead=(j,)), new_state, new_buf


def hgrn_sample(u, j, layer, state, w_in, lb_logits, norm_g, w_out):
    b, d = u.shape
    _, heads, dk, dv = state.shape
    qfig = mm(u, w_in, n=4 * d, tn=512, lead=(j,)).reshape(b, 4, heads, dk)
    lb_p = jax.nn.softmax(lb_logits, axis=0)
    lb = (jnp.cumsum(lb_p, axis=0) - lb_p[0])[layer].reshape(heads, dk)
    fg = lb + (1.0 - lb) * jax.nn.sigmoid(qfig[:, 1])
    q = _silu(qfig[:, 0]) * (dk ** -0.5)
    o, new_state = hgrn_step(q, 1.0 - fg, jnp.exp(jnp.log(fg)), qfig[:, 2], qfig[:, 3], norm_g[j], state)
    return mm(o.reshape(b, d), w_out, n=d, tn=512, lead=(j,)), new_state


def mla_sample(u, j, past_len, cache, page_table, w_a, q_norm, kv_norm, w_qb, w_kvb, w_o):
    b, d = u.shape
    kv_rank = kv_norm.shape[-1]
    pos = jnp.full((1,), past_len, jnp.int32)
    q_nope, q_pe, ckv, k_pe, rope_dim = mla_project(u, pos, j, w_a, q_norm, kv_norm, w_qb)
    new_rows = jnp.concatenate([ckv, k_pe[:, :rope_dim]], axis=-1)
    q_lat = jnp.swapaxes(latent_in(q_nope, w_kvb, j), 0, 1)
    q_cat = jnp.concatenate([q_lat.astype(BF16), q_pe.reshape(b, ML_HEADS, rope_dim)], axis=-1)
    o_lat = mla_decode(q_cat, new_rows[:, None, :], jnp.swapaxes(cache, 2, 3), j, page_table, kv_rank)
    o = latent_out(jnp.swapaxes(o_lat, 0, 1), w_kvb, j)
    return mm(o, w_o, n=d, tn=512, lead=(j,)), new_rows[:, None, :]


N_MIXERS = 4


_STATE_KEYS = ("ssm", "ssm_conv", "gdn", "gdn_conv", "hgrn", "mla")


def _trunk(xp, xs, bp, tp, st, p):
    norm_g = p["norm_g"]
    depth = norm_g.shape[0]
    outs_p = {k: [] for k in _STATE_KEYS}
    outs_s = {k: [] for k in _STATE_KEYS}
    ffn_w = (p["ffn_w_gate"], p["ffn_w_up"], p["ffn_w_down"])
    hp, hs = xp, xs
    xnp, xns = prenorm(hp, norm_g[0, 0]), prenorm(hs, norm_g[0, 0])
    for i in range(depth):
        mix_id, j = i % N_MIXERS, i // N_MIXERS
        g = norm_g[i]
        yp, ys = ffn(xnp, xns, *ffn_w, i, 0)
        hp, up = sandwich(hp, yp, g[1], g[2], 0.5)
        hs, us = sandwich(hs, ys, g[1], g[2], 0.5)
        if mix_id == 0:
            args = (p["mb_w_in"], p["mb_conv_w"], p["mb_conv_b"], p["mb_dt_bias"], p["mb_a_log"],
                    p["mb_d"], p["mb_norm"], p["mb_w_out"])
            mix_p, s1, c1 = mamba_prompt(up, bp, tp, j, *args)
            outs_p["ssm"].append(s1)
            outs_p["ssm_conv"].append(c1)
            mix_s, s1, c1 = mamba_sample(us, j, st["ssm_conv"][j], st["ssm"][j], *args)
            outs_s["ssm"].append(s1)
            outs_s["ssm_conv"].append(c1)
        elif mix_id == 1:
            args = (p["gd_w_in"], p["gd_conv_w"], p["gd_dt_bias"], p["gd_a_log"], p["gd_norm"],
                    p["gd_w_out"])
            mix_p, s1, c1 = gdn_prompt_mixer(up, bp, tp, j, *args)
            outs_p["gdn"].append(s1)
            outs_p["gdn_conv"].append(c1)
            mix_s, s1, c1 = gdn_sample(us, j, st["gdn_conv"][j], st["gdn"][j], *args)
            outs_s["gdn"].append(s1)
            outs_s["gdn_conv"].append(c1)
        elif mix_id == 2:
            args = (p["hg_w_in"], p["hg_lb_logits"], p["hg_norm"], p["hg_w_out"])
            mix_p, s1 = hgrn_prompt_mixer(up, bp, tp, j, i, *args)
            outs_p["hgrn"].append(s1)
            mix_s, s1 = hgrn_sample(us, j, i, st["hgrn"][j], *args)
            outs_s["hgrn"].append(s1)
        else:
            args = (p["ml_w_a"], p["ml_q_norm"], p["ml_kv_norm"], p["ml_w_qb"], p["ml_w_kvb"],
                    p["ml_w_o"])
            mix_p, rows = mla_prompt_mixer(up, bp, tp, j, *args)
            outs_p["mla"].append(rows)
            mix_s, rows = mla_sample(us, j, st["past_len"], st["cache"], st["page_table"], *args)
            outs_s["mla"].append(rows)
        hp, xnp = sandwich(hp, mix_p, g[3], g[4], 1.0)
        hs, xns = sandwich(hs, mix_s, g[3], g[4], 1.0)
        yp, ys = ffn(xnp, xns, *ffn_w, i, 1)
        g_next = norm_g[i + 1, 0] if i + 1 < depth else None
        hp, xnp = sandwich(hp, yp, g[5], g_next, 0.5)
        hs, xns = sandwich(hs, ys, g[5], g_next, 0.5)
    stack = lambda outs: tuple(jnp.stack(outs[k]) for k in _STATE_KEYS)
    return (hp, hs) + stack(outs_p) + stack(outs_s)


def kernel(x_prompt, x_sample, state_ssm, state_ssm_conv, state_gdn, state_gdn_conv, state_hgrn, cache_mla, page_table, norm_g, ffn_w_gate, ffn_w_up, ffn_w_down, mb_w_in, mb_conv_w, mb_conv_b, mb_dt_bias, mb_a_log, mb_d, mb_norm, mb_w_out, gd_w_in, gd_conv_w, gd_dt_bias, gd_a_log, gd_norm, gd_w_out, hg_w_in, hg_lb_logits, hg_norm, hg_w_out, ml_w_a, ml_q_norm, ml_kv_norm, ml_w_qb, ml_w_kvb, ml_w_o):
    p = dict(norm_g=norm_g, ffn_w_gate=ffn_w_gate, ffn_w_up=ffn_w_up, ffn_w_down=ffn_w_down,
             mb_w_in=mb_w_in, mb_conv_w=mb_conv_w, mb_conv_b=mb_conv_b, mb_dt_bias=mb_dt_bias,
             mb_a_log=mb_a_log, mb_d=mb_d, mb_norm=mb_norm, mb_w_out=mb_w_out,
             gd_w_in=gd_w_in, gd_conv_w=gd_conv_w, gd_dt_bias=gd_dt_bias, gd_a_log=gd_a_log,
             gd_norm=gd_norm, gd_w_out=gd_w_out,
             hg_w_in=hg_w_in, hg_lb_logits=hg_lb_logits, hg_norm=hg_norm, hg_w_out=hg_w_out,
             ml_w_a=ml_w_a, ml_q_norm=ml_q_norm, ml_kv_norm=ml_kv_norm, ml_w_qb=ml_w_qb,
             ml_w_kvb=ml_w_kvb, ml_w_o=ml_w_o)
    bp, tp, d = x_prompt.shape
    bs, ts, _ = x_sample.shape
    assert ts == 1
    st = dict(ssm=state_ssm, ssm_conv=state_ssm_conv, gdn=state_gdn, gdn_conv=state_gdn_conv,
              hgrn=state_hgrn, cache=cache_mla, page_table=page_table,
              past_len=page_table.shape[1] * cache_mla.shape[2])
    out = _trunk(x_prompt.reshape(bp * tp, d), x_sample.reshape(bs * ts, d), bp, tp, st, p)
    return (out[0].reshape(bp, tp, d), out[1].reshape(bs, ts, d)) + out[2:]
```
